```python
import math
import jax, jax.numpy as jnp
from jax import lax
import numpy as np

D_MODEL = 1024
BATCH = 16
SEQ = 2048
DEPTH = 4

N_MIXERS = 2
N_ATTN_LAYERS = (DEPTH + 1) // 2
N_RET_LAYERS = DEPTH // 2
NORM_EPS = 1e-6
N_NORMS = 6

D_FF = 2816

DILATED_GROUPS = ((128, 1), (512, 4), (2048, 16))
N_GROUPS = len(DILATED_GROUPS)
H_A = 16
DH_A = 64
D_A = H_A * DH_A
ATTN_IN_COLS = N_GROUPS * 3 * D_A
BLK = 64
NEG_INF = -1e30

NUM_BUCKETS = 32
REL_MAX_DISTANCE = 1024

H_R = 4
DK_R = D_MODEL // H_R
DV_R = 2 * D_MODEL // H_R
D_V = H_R * DV_R
RET_IN_COLS = 2 * H_R * DK_R + 3 * D_V
RET_CHUNK = 128
ROPE_BASE = 10000.0

kernel_name = "hybrid_dilated_attn_retention_macaron"


def rms_norm(x, g):
    xf = x.astype(jnp.float32)
    y = xf * lax.rsqrt(jnp.mean(xf * xf, axis=-1, keepdims=True) + NORM_EPS)
    return (y * g.astype(jnp.float32)).astype(x.dtype)


def swiglu(h, w_gate, w_up, w_down):
    return (jax.nn.silu(h @ w_gate) * (h @ w_up)) @ w_down


def t5_buckets(rel):
    half = NUM_BUCKETS // 2
    max_exact = half // 2
    n = np.abs(rel)
    large = max_exact + (np.log(np.maximum(n, 1) / max_exact)
                         / np.log(REL_MAX_DISTANCE / max_exact) * (half - max_exact)).astype(np.int64)
    large = np.minimum(large, half - 1)
    return ((rel > 0) * half + np.where(n < max_exact, n, large)).astype(np.int32)


def banded_attention(q, k, v, bias, radius):
    N, L, H, dh = q.shape
    nb = -(-L // BLK)
    Lp = nb * BLK
    qb = jnp.pad(q, ((0, 0), (0, Lp - L), (0, 0), (0, 0))).reshape(N, nb, BLK, H, dh)

    def windows(t):
        tp = jnp.pad(t, ((0, 0), (BLK, Lp - L + BLK), (0, 0), (0, 0))).reshape(N, nb + 2, BLK, H, dh)
        return jnp.concatenate([tp[:, :-2], tp[:, 1:-1], tp[:, 2:]], axis=2)

    kw, vw = windows(k), windows(v)
    a = np.arange(BLK)[:, None]
    c = np.arange(3 * BLK)[None, :]
    off = c - BLK - a
    key_pos = np.arange(nb)[:, None, None] * BLK - BLK + c[None]
    valid = (np.abs(off) <= radius)[None] & (key_pos >= 0) & (key_pos < L)
    bias_blk = bias[:, np.clip(off + radius, 0, 2 * radius)].astype(jnp.float32)

    s = jnp.einsum('nbqhd,nbkhd->nbhqk', qb, kw).astype(jnp.float32) * (dh ** -0.5) + bias_blk
    s = jnp.where(valid[None, :, None], s, NEG_INF)
    smax = jnp.max(s, axis=-1, keepdims=True)
    p = jnp.exp(s - smax)
    denom = jnp.sum(p, axis=-1, keepdims=True)
    o = jnp.einsum('nbhqk,nbkhd->nbqhd', (p / denom).astype(v.dtype), vw)
    lse = (smax + jnp.log(denom))[..., 0]
    o = o.reshape(N, Lp, H, dh)[:, :L]
    lse = lse.transpose(0, 1, 3, 2).reshape(N, Lp, H)[:, :L]
    return o, lse


def dilated_group(q, k, v, bias, dilation, radius):
    B, S, H, dh = q.shape
    Ls = S // dilation

    def split(t):
        return t.reshape(B, Ls, dilation, H, dh).transpose(0, 2, 1, 3, 4).reshape(B * dilation, Ls, H, dh)

    o, lse = banded_attention(split(q), split(k), split(v), bias, radius)
    o = o.reshape(B, dilation, Ls, H, dh).transpose(0, 2, 1, 3, 4).reshape(B, S, H, dh)
    lse = lse.reshape(B, dilation, Ls, H).transpose(0, 2, 1, 3).reshape(B, S, H)
    return o, lse


def dilated_attention(h, w_in, w_out, rel_bias):
    B, S, _ = h.shape
    proj = (h @ w_in).reshape(B, S, N_GROUPS, 3, H_A, DH_A)
    outs, lses = [], []
    for g, (window, dilation) in enumerate(DILATED_GROUPS):
        radius = window // (2 * dilation)
        buckets = t5_buckets(np.arange(-radius, radius + 1) * dilation)
        bias_g = rel_bias[g * H_A:(g + 1) * H_A][:, buckets]
        o, lse = dilated_group(proj[:, :, g, 0], proj[:, :, g, 1], proj[:, :, g, 2],
                               bias_g, dilation, radius)
        outs.append(o)
        lses.append(lse)
    wts = jax.nn.softmax(jnp.stack(lses, axis=0), axis=0)
    o = jnp.einsum('gbsh,gbshd->bshd', wts.astype(h.dtype), jnp.stack(outs, axis=0))
    return o.reshape(B, S, D_A) @ w_out


def rotary(t, pos):
    half = t.shape[-1] // 2
    inv_freq = 1.0 / (ROPE_BASE ** jnp.linspace(0.0, 1.0, half, dtype=jnp.float32))
    ang = pos[:, None] * inv_freq[None, :]
    cos = jnp.cos(ang)[None, :, None, :]
    sin = jnp.sin(ang)[None, :, None, :]
    t1, t2 = t[..., :half], t[..., half:]
    return jnp.concatenate([t1 * cos - t2 * sin, t1 * sin + t2 * cos], axis=-1)


def chunk_retention(q, k, v, log_gamma):
    B, S, H, dk = q.shape
    dv = v.shape[-1]
    C = RET_CHUNK
    n = S // C

    def to_chunks(t):
        return t.reshape(B, n, C, H, t.shape[-1]).transpose(1, 0, 3, 2, 4)

    qc, kc, vc = to_chunks(q), to_chunks(k), to_chunks(v)
    i = jnp.arange(C, dtype=jnp.float32)
    rel = i[:, None] - i[None, :]
    dmat = jnp.where(rel >= 0, jnp.exp(log_gamma[:, None, None] * jnp.maximum(rel, 0.0)), 0.0)
    inner = jnp.einsum('nbhid,nbhjd->nbhij', qc, kc) * dmat
    inner = jnp.einsum('nbhij,nbhje->nbhie', inner, vc)
    q_decay = jnp.exp(log_gamma[:, None] * (i + 1.0))[None, :, :, None]
    k_decay = jnp.exp(log_gamma[:, None] * (C - 1.0 - i))[None, :, :, None]
    chunk_decay = jnp.exp(log_gamma * C)[None, :, None, None]

    def step(state, xs):
        qi, ki, vi = xs
        cross = jnp.einsum('bhid,bhde->bhie', qi, state) * q_decay
        state = state * chunk_decay + jnp.einsum('bhjd,bhje->bhde', ki * k_decay, vi)
        return state, cross

    _, cross = lax.scan(step, jnp.zeros((B, H, dk, dv), jnp.float32), (qc, kc, vc))
    y = inner + cross
    return y.transpose(1, 0, 3, 2, 4).reshape(B, S, H, dv)


def head_group_norm(y):
    mu = jnp.mean(y, axis=-1, keepdims=True)
    var = jnp.mean(jnp.square(y - mu), axis=-1, keepdims=True)
    return (y - mu) * lax.rsqrt(var + NORM_EPS)


def retention(h, w_in, w_out, decay_logit):
    B, S, _ = h.shape
    proj = (h @ w_in).astype(jnp.float32)
    dq = H_R * DK_R
    q = proj[..., :dq].reshape(B, S, H_R, DK_R)
    k = proj[..., dq:2 * dq].reshape(B, S, H_R, DK_R)
    v = proj[..., 2 * dq:2 * dq + D_V].reshape(B, S, H_R, DV_R)
    g_f = proj[..., 2 * dq + D_V:2 * dq + 2 * D_V].reshape(B, S, H_R, DV_R)
    g_b = proj[..., 2 * dq + 2 * D_V:].reshape(B, S, H_R, DV_R)
    pos = jnp.arange(S, dtype=jnp.float32)
    q = rotary(q, pos)
    k = rotary(k, pos) * (DK_R ** -0.5)
    log_gamma = jnp.log1p(-jnp.exp(decay_logit.astype(jnp.float32)))
    y_f = chunk_retention(q, k, v, log_gamma[0])
    y_b = jnp.flip(chunk_retention(jnp.flip(q, 1), jnp.flip(k, 1), jnp.flip(v, 1), log_gamma[1]), 1)
    y = jax.nn.silu(g_f) * head_group_norm(y_f) + jax.nn.silu(g_b) * head_group_norm(y_b)
    return y.reshape(B, S, D_V).astype(h.dtype) @ w_out


def setup_inputs(seed: int = 0) -> dict:
    key = jax.random.key(seed)
    ks = jax.random.split(key, 12)
    f32 = jnp.float32
    x = jax.random.normal(ks[0], (BATCH, SEQ, D_MODEL), f32)
    norm_gains = 1.0 + 0.05 * jax.random.normal(ks[1], (DEPTH, N_NORMS, D_MODEL), f32)
    ffn_w_gate = jax.random.normal(ks[2], (DEPTH, 2, D_MODEL, D_FF), f32) * D_MODEL ** -0.5
    ffn_w_up = jax.random.normal(ks[3], (DEPTH, 2, D_MODEL, D_FF), f32) * D_MODEL ** -0.5
    ffn_w_down = jax.random.normal(ks[4], (DEPTH, 2, D_FF, D_MODEL), f32) * D_FF ** -0.5
    attn_w_in = jax.random.normal(ks[5], (N_ATTN_LAYERS, D_MODEL, ATTN_IN_COLS), f32) * D_MODEL ** -0.5
    attn_w_out = jax.random.normal(ks[6], (N_ATTN_LAYERS, D_A, D_MODEL), f32) * D_A ** -0.5
    rel_bias = 0.5 * jax.random.normal(ks[7], (N_GROUPS * H_A, NUM_BUCKETS), f32)
    ret_w_in = jax.random.normal(ks[8], (N_RET_LAYERS, D_MODEL, RET_IN_COLS), f32) * D_MODEL ** -0.5
    ret_w_out = jax.random.normal(ks[9], (N_RET_LAYERS, D_V, D_MODEL), f32) * D_V ** -0.5
    base = -(5.0 + jnp.arange(H_R, dtype=f32)) * math.log(2.0)
    ret_decay_logit = base[None, None, :] + 0.1 * jax.random.normal(ks[10], (N_RET_LAYERS, 2, H_R), f32)
    return {"x": x, "norm_gains": norm_gains, "ffn_w_gate": ffn_w_gate, "ffn_w_up": ffn_w_up,
            "ffn_w_down": ffn_w_down, "attn_w_in": attn_w_in, "attn_w_out": attn_w_out,
            "rel_bias": rel_bias, "ret_w_in": ret_w_in, "ret_w_out": ret_w_out,
            "ret_decay_logit": ret_decay_logit}


def reference(x, norm_gains, ffn_w_gate, ffn_w_up, ffn_w_down, attn_w_in, attn_w_out,
              rel_bias, ret_w_in, ret_w_out, ret_decay_logit):
    for i in range(DEPTH):
        g = norm_gains[i]
        h = swiglu(rms_norm(x, g[0]), ffn_w_gate[i, 0], ffn_w_up[i, 0], ffn_w_down[i, 0])
        x = x + 0.5 * rms_norm(h, g[1])
        hm = rms_norm(x, g[2])
        if i % N_MIXERS == 0:
            j = i // N_MIXERS
            m = dilated_attention(hm, attn_w_in[j], attn_w_out[j], rel_bias)
        else:
            j = i // N_MIXERS
            m = retention(hm, ret_w_in[j], ret_w_out[j], ret_decay_logit[j])
        x = x + rms_norm(m, g[3])
        h = swiglu(rms_norm(x, g[4]), ffn_w_gate[i, 1], ffn_w_up[i, 1], ffn_w_down[i, 1])
        x = x + 0.5 * rms_norm(h, g[5])
    return x
```

```python
import functools
import math

import jax
import jax.numpy as jnp
import numpy as np
from jax import lax
from jax.experimental import pallas as pl
from jax.experimental.pallas import tpu as pltpu

D_MODEL = 1024
DEPTH = 4
NORM_EPS = 1e-6
D_FF = 2816
DILATED_GROUPS = ((128, 1), (512, 4), (2048, 16))
N_GROUPS = len(DILATED_GROUPS)
H_A = 16
DH_A = 64
D_A = H_A * DH_A
ATTN_IN_COLS = N_GROUPS * 3 * D_A
NEG_INF = -1e30
NUM_BUCKETS = 32
REL_MAX_DISTANCE = 1024
H_R = 4
DK_R = D_MODEL // H_R
DV_R = 2 * D_MODEL // H_R
D_V = H_R * DV_R
RET_IN_COLS = 2 * H_R * DK_R + 3 * D_V
RET_CHUNK = 128
ROPE_BASE = 10000.0

V7X_LANES = 128
V7X_VMEM_LIMIT_BYTES = 56 * 1024 * 1024

FFN_ROWS = 512
FFN_COLS = 256
PROJ_ROWS = 512
PROJ_COLS = 1024
ATTN_Q = 128
ATTN_RADIUS = 64
OUT_ROWS = 512

BF16 = jnp.bfloat16
F32 = jnp.float32


def _params(*semantics):
    return pltpu.CompilerParams(dimension_semantics=semantics,
                                vmem_limit_bytes=V7X_VMEM_LIMIT_BYTES)


def _resident(shape, index_map):
    return pl.BlockSpec(shape, index_map, pipeline_mode=pl.Buffered(1))


def _rms(x, g):
    ms = jnp.mean(x * x, axis=-1, keepdims=True)
    return x * lax.rsqrt(ms + NORM_EPS) * g


def _dot(a, b):
    return jnp.dot(a, b, preferred_element_type=F32)


def _ffn_kernel(x_ref, gpre_ref, wg_ref, wu_ref, wd_ref, gpost_ref, o_ref, acc_ref):
    x = x_ref[...]
    h = _rms(x, gpre_ref[...]).astype(BF16)
    for c in range(D_FF // FFN_COLS):
        cols = slice(c * FFN_COLS, (c + 1) * FFN_COLS)
        gate = _dot(h, wg_ref[:, cols])
        up = _dot(h, wu_ref[:, cols])
        act = (gate * jax.nn.sigmoid(gate) * up).astype(BF16)
        part = _dot(act, wd_ref[cols, :])
        if c == 0:
            acc_ref[...] = part
        else:
            acc_ref[...] += part
    o_ref[...] = x + 0.5 * _rms(acc_ref[...], gpost_ref[...])


def _ffn(x2d, g_pre, w_gate, w_up, w_down, g_post):
    t = x2d.shape[0]
    row = lambda i: (i, 0)
    fixed = lambda i: (0, 0)
    return pl.pallas_call(
        _ffn_kernel,
        out_shape=jax.ShapeDtypeStruct((t, D_MODEL), F32),
        grid=(t // FFN_ROWS,),
        in_specs=[
            pl.BlockSpec((FFN_ROWS, D_MODEL), row),
            _resident((1, D_MODEL), fixed),
            _resident((D_MODEL, D_FF), fixed),
            _resident((D_MODEL, D_FF), fixed),
            _resident((D_FF, D_MODEL), fixed),
            _resident((1, D_MODEL), fixed),
        ],
        out_specs=pl.BlockSpec((FFN_ROWS, D_MODEL), row),
        scratch_shapes=[pltpu.VMEM((FFN_ROWS, D_MODEL), F32)],
        compiler_params=_params("parallel"),
        name="ffn",
    )(x2d, g_pre, w_gate, w_up, w_down, g_post)


def _norm_proj_kernel(x_ref, g_ref, w_ref, o_ref):
    h = _rms(x_ref[...], g_ref[...]).astype(BF16)
    n = w_ref.shape[1]
    for c in range(n // PROJ_COLS):
        cols = slice(c * PROJ_COLS, (c + 1) * PROJ_COLS)
        o_ref[:, cols] = _dot(h, w_ref[:, cols]).astype(BF16)


def _norm_proj(x2d, g, w_in):
    t = x2d.shape[0]
    n = w_in.shape[1]
    row = lambda i: (i, 0)
    fixed = lambda i: (0, 0)
    return pl.pallas_call(
        _norm_proj_kernel,
        out_shape=jax.ShapeDtypeStruct((t, n), BF16),
        grid=(t // PROJ_ROWS,),
        in_specs=[
            pl.BlockSpec((PROJ_ROWS, D_MODEL), row),
            _resident((1, D_MODEL), fixed),
            _resident((D_MODEL, n), fixed),
        ],
        out_specs=pl.BlockSpec((PROJ_ROWS, n), row),
        compiler_params=_params("parallel"),
        name="norm_proj",
    )(x2d, g, w_in)


def _t5_buckets(rel):
    half = NUM_BUCKETS // 2
    max_exact = half // 2
    n = np.abs(rel)
    large = max_exact + (np.log(np.maximum(n, 1) / max_exact)
                         / np.log(REL_MAX_DISTANCE / max_exact) * (half - max_exact)).astype(np.int64)
    large = np.minimum(large, half - 1)
    return ((rel > 0) * half + np.where(n < max_exact, n, large)).astype(np.int32)


def _attn_window(ls):
    return min(ls, ATTN_Q + 2 * ATTN_RADIUS)


def _window_shifts(ls):
    w = _attn_window(ls)
    if w == ls:
        return (0,)
    return (0, -ATTN_RADIUS, ATTN_Q - w)


def _bias_tiles(rel_bias, g, dilation, ls):
    w = _attn_window(ls)
    buckets = _t5_buckets(np.arange(-ATTN_RADIUS, ATTN_RADIUS + 1) * dilation)
    bias_g = rel_bias[g * H_A:(g + 1) * H_A][:, buckets].astype(F32)
    a = np.arange(ATTN_Q)[:, None]
    c = np.arange(w)[None, :]
    tiles = []
    for shift in _window_shifts(ls):
        off = shift + c - a
        valid = np.abs(off) <= ATTN_RADIUS
        idx = np.clip(off + ATTN_RADIUS, 0, 2 * ATTN_RADIUS)
        tiles.append(jnp.where(valid[None], bias_g[:, idx], NEG_INF))
    return jnp.stack(tiles, axis=0)


def _attn_kernel(q_ref, k_ref, v_ref, bias_ref, o_ref, lse_ref, *, ls):
    w = _attn_window(ls)
    n_blocks = ls // ATTN_Q
    i = pl.program_id(2)
    if w == ls:
        start = 0
        variant = 0
    else:
        start = pl.multiple_of(jnp.clip(i * ATTN_Q - ATTN_RADIUS, 0, ls - w), ATTN_RADIUS)
        variant = jnp.where(i == 0, 0, jnp.where(i == n_blocks - 1, 2, 1))
    lane = lax.broadcasted_iota(jnp.int32, (ATTN_Q, V7X_LANES), 1)
    lse_tile = jnp.zeros((ATTN_Q, V7X_LANES), F32)
    for pair in range(H_A * DH_A // V7X_LANES):
        cols = slice(pair * V7X_LANES, (pair + 1) * V7X_LANES)
        qp = q_ref[0, :, cols] * jnp.asarray(DH_A ** -0.5, BF16)
        kp = k_ref[0, pl.ds(start, w), cols]
        vp = v_ref[0, pl.ds(start, w), cols]
        out = jnp.zeros((ATTN_Q, V7X_LANES), F32)
        for sub in range(V7X_LANES // DH_A):
            head = pair * (V7X_LANES // DH_A) + sub
            mine = (lane >= sub * DH_A) & (lane < (sub + 1) * DH_A)
            qh = jnp.where(mine, qp, jnp.zeros_like(qp))
            s = lax.dot_general(qh, kp, (((1,), (1,)), ((), ())), preferred_element_type=F32)
            s = s + bias_ref[variant, head]
            m = jnp.max(s, axis=-1, keepdims=True)
            e = jnp.exp(s - m)
            denom = jnp.sum(e, axis=-1, keepdims=True)
            pv = _dot(e.astype(BF16), vp)
            out = jnp.where(mine, pv / denom, out)
            lse_tile = jnp.where(lane == head, m + jnp.log(denom), lse_tile)
        o_ref[0, :, cols] = out.astype(BF16)
    lse_ref[0] = lse_tile


def _attn_group(proj, bias_tiles, g, dilation):
    b, s, _ = proj.shape
    ls = s // dilation
    w = _attn_window(ls)
    nvar = bias_tiles.shape[0]
    proj_v = proj.reshape(b, ls, dilation * ATTN_IN_COLS)
    per_row = ATTN_IN_COLS // D_A
    qmap = lambda bi, r, i: (bi, i, r * per_row + 3 * g)
    kmap = lambda bi, r, i: (bi, 0, r * per_row + 3 * g + 1)
    vmap = lambda bi, r, i: (bi, 0, r * per_row + 3 * g + 2)
    omap = lambda bi, r, i: (bi, i, r)
    out, lse = pl.pallas_call(
        functools.partial(_attn_kernel, ls=ls),
        out_shape=(jax.ShapeDtypeStruct((b, ls, dilation * D_A), BF16),
                   jax.ShapeDtypeStruct((b, ls, dilation * V7X_LANES), F32)),
        grid=(b, dilation, ls // ATTN_Q),
        in_specs=[
            pl.BlockSpec((1, ATTN_Q, D_A), qmap),
            pl.BlockSpec((1, ls, D_A), kmap),
            pl.BlockSpec((1, ls, D_A), vmap),
            _resident((nvar, H_A, ATTN_Q, w), lambda bi, r, i: (0, 0, 0, 0)),
        ],
        out_specs=(pl.BlockSpec((1, ATTN_Q, D_A), omap),
                   pl.BlockSpec((1, ATTN_Q, V7X_LANES), omap)),
        compiler_params=_params("parallel", "parallel", "arbitrary"),
        name=f"attn_group{g}",
    )(proj_v, proj_v, proj_v, bias_tiles)
    return out.reshape(b, s, D_A), lse.reshape(b, s, V7X_LANES)


def _attn_out_kernel(x_ref, o0_ref, o1_ref, o2_ref, l0_ref, l1_ref, l2_ref,
                     expand_ref, w_ref, g_ref, out_ref):
    lses = [l0_ref[...], l1_ref[...], l2_ref[...]]
    outs = [o0_ref, o1_ref, o2_ref]
    m = jnp.maximum(jnp.maximum(lses[0], lses[1]), lses[2])
    es = [jnp.exp(l - m) for l in lses]
    denom = es[0] + es[1] + es[2]
    merged = None
    for e, o_ref in zip(es, outs):
        wt = (e / denom).astype(BF16)
        wt_wide = _dot(wt, expand_ref[...])
        term = wt_wide * o_ref[...].astype(F32)
        merged = term if merged is None else merged + term
    y = _dot(merged.astype(BF16), w_ref[...])
    out_ref[...] = x_ref[...] + _rms(y, g_ref[...])


def _attn_out(x2d, outs, lses, w_out, g):
    t = x2d.shape[0]
    head_of_col = np.arange(D_A) // DH_A
    expand = jnp.asarray(np.arange(V7X_LANES)[:, None] == head_of_col[None, :], BF16)
    row = lambda i: (i, 0)
    fixed = lambda i: (0, 0)
    return pl.pallas_call(
        _attn_out_kernel,
        out_shape=jax.ShapeDtypeStruct((t, D_MODEL), F32),
        grid=(t // OUT_ROWS,),
        in_specs=[pl.BlockSpec((OUT_ROWS, D_MODEL), row)]
        + [pl.BlockSpec((OUT_ROWS, D_A), row)] * N_GROUPS
        + [pl.BlockSpec((OUT_ROWS, V7X_LANES), row)] * N_GROUPS
        + [_resident((V7X_LANES, D_A), fixed),
           _resident((D_A, D_MODEL), fixed),
           _resident((1, D_MODEL), fixed)],
        out_specs=pl.BlockSpec((OUT_ROWS, D_MODEL), row),
        compiler_params=_params("parallel"),
        name="attn_out",
    )(x2d, *outs, *lses, expand, w_out, g)


def _rotate(t_ref, rows, cos, sin, scale):
    half = DK_R // 2
    t1 = t_ref[0, rows, :half].astype(F32)
    t2 = t_ref[0, rows, half:].astype(F32)
    out = jnp.concatenate([t1 * cos - t2 * sin, t1 * sin + t2 * cos], axis=-1)
    if scale != 1.0:
        out = out * scale
    return out


def _group_norm(y):
    mu = jnp.mean(y, axis=-1, keepdims=True)
    d = y - mu
    var = jnp.mean(d * d, axis=-1, keepdims=True)
    return d * lax.rsqrt(var + NORM_EPS)


def _retention_kernel(lg_ref, q_ref, k_ref, v_ref, gf_ref, gb_ref, cos_ref, sin_ref,
                      o_ref, state_ref, yf_ref):
    c_len = RET_CHUNK
    s_len = q_ref.shape[1]
    n_chunks = s_len // c_len
    head = pl.program_id(1)
    ii = lax.broadcasted_iota(jnp.int32, (c_len, c_len), 0).astype(F32)
    jj = lax.broadcasted_iota(jnp.int32, (c_len, c_len), 1).astype(F32)
    col = lax.broadcasted_iota(jnp.int32, (c_len, 1), 0).astype(F32)

    def scan(direction):
        lg = lg_ref[direction, head]
        if direction == 0:
            rel = ii - jj
            q_pow = col + 1.0
            k_pow = (c_len - 1.0) - col
        else:
            rel = jj - ii
            q_pow = c_len - col
            k_pow = col
        dmat = jnp.where(rel >= 0, jnp.exp(lg * jnp.maximum(rel, 0.0)), 0.0)
        q_decay = jnp.exp(lg * q_pow)
        k_decay = jnp.exp(lg * k_pow)
        chunk_decay = jnp.exp(lg * c_len)
        gate_ref = gf_ref if direction == 0 else gb_ref
        state_ref[...] = jnp.zeros_like(state_ref)

        def body(n, carry):
            chunk = n if direction == 0 else n_chunks - 1 - n
            rows = pl.ds(pl.multiple_of(chunk * c_len, c_len), c_len)
            cos = cos_ref[rows, :]
            sin = sin_ref[rows, :]
            q = _rotate(q_ref, rows, cos, sin, 1.0)
            k = _rotate(k_ref, rows, cos, sin, DK_R ** -0.5)
            v = v_ref[0, rows, :]
            qb = q.astype(BF16)
            scores = lax.dot_general(qb, k.astype(BF16), (((1,), (1,)), ((), ())),
                                     preferred_element_type=F32) * dmat
            inner = _dot(scores.astype(BF16), v)
            state = state_ref[...]
            cross = _dot(qb, state.astype(BF16)) * q_decay
            kd = (k * k_decay).astype(BF16)
            state_ref[...] = state * chunk_decay + lax.dot_general(
                kd, v, (((0,), (0,)), ((), ())), preferred_element_type=F32)
            gate = gate_ref[0, rows, :].astype(F32)
            y = gate * jax.nn.sigmoid(gate) * _group_norm(inner + cross)
            if direction == 0:
                yf_ref[rows, :] = y
            else:
                o_ref[0, rows, :] = (yf_ref[rows, :] + y).astype(BF16)
            return carry

        lax.fori_loop(0, n_chunks, body, 0)

    scan(0)
    scan(1)


def _retention(proj, log_gamma, cos, sin):
    b, s, _ = proj.shape
    q0 = 0
    k0 = H_R
    v0 = (2 * H_R * DK_R) // DV_R
    gf0 = v0 + H_R
    gb0 = gf0 + H_R
    return pl.pallas_call(
        _retention_kernel,
        out_shape=jax.ShapeDtypeStruct((b, s, D_V), BF16),
        grid_spec=pltpu.PrefetchScalarGridSpec(
            num_scalar_prefetch=1,
            grid=(b, H_R),
            in_specs=[
                pl.BlockSpec((1, s, DK_R), lambda bi, h, lg: (bi, 0, q0 + h)),
                pl.BlockSpec((1, s, DK_R), lambda bi, h, lg: (bi, 0, k0 + h)),
                pl.BlockSpec((1, s, DV_R), lambda bi, h, lg: (bi, 0, v0 + h)),
                pl.BlockSpec((1, s, DV_R), lambda bi, h, lg: (bi, 0, gf0 + h)),
                pl.BlockSpec((1, s, DV_R), lambda bi, h, lg: (bi, 0, gb0 + h)),
                _resident((s, DK_R // 2), lambda bi, h, lg: (0, 0)),
                _resident((s, DK_R // 2), lambda bi, h, lg: (0, 0)),
            ],
            out_specs=pl.BlockSpec((1, s, DV_R), lambda bi, h, lg: (bi, 0, h)),
            scratch_shapes=[pltpu.VMEM((DK_R, DV_R), F32), pltpu.VMEM((s, DV_R), F32)],
        ),
        compiler_params=_params("parallel", "parallel"),
        name="retention",
    )(log_gamma, proj, proj, proj, proj, proj, cos, sin)


def _proj_out_kernel(x_ref, y_ref, w_ref, g_ref, out_ref):
    out_ref[...] = x_ref[...] + _rms(_dot(y_ref[...], w_ref[...]), g_ref[...])


def _proj_out(x2d, y2d, w_out, g):
    t = x2d.shape[0]
    k = y2d.shape[1]
    row = lambda i: (i, 0)
    fixed = lambda i: (0, 0)
    return pl.pallas_call(
        _proj_out_kernel,
        out_shape=jax.ShapeDtypeStruct((t, D_MODEL), F32),
        grid=(t // OUT_ROWS,),
        in_specs=[
            pl.BlockSpec((OUT_ROWS, D_MODEL), row),
            pl.BlockSpec((OUT_ROWS, k), row),
            _resident((k, D_MODEL), fixed),
            _resident((1, D_MODEL), fixed),
        ],
        out_specs=pl.BlockSpec((OUT_ROWS, D_MODEL), row),
        compiler_params=_params("parallel"),
        name="proj_out",
    )(x2d, y2d, w_out, g)


def _rope_tables(s):
    half = DK_R // 2
    inv_freq = 1.0 / (ROPE_BASE ** jnp.linspace(0.0, 1.0, half, dtype=F32))
    ang = jnp.arange(s, dtype=F32)[:, None] * inv_freq[None, :]
    return jnp.cos(ang), jnp.sin(ang)


def kernel(x, norm_gains, ffn_w_gate, ffn_w_up, ffn_w_down, attn_w_in, attn_w_out,
           rel_bias, ret_w_in, ret_w_out, ret_decay_logit):
    b, s, d = x.shape
    t = b * s
    gains = norm_gains.astype(F32).reshape(DEPTH, -1, 1, D_MODEL)
    w_gate = ffn_w_gate.astype(BF16)
    w_up = ffn_w_up.astype(BF16)
    w_down = ffn_w_down.astype(BF16)
    a_w_in = attn_w_in.astype(BF16)
    a_w_out = attn_w_out.astype(BF16)
    r_w_in = ret_w_in.astype(BF16)
    r_w_out = ret_w_out.astype(BF16)
    bias_tiles = [_bias_tiles(rel_bias, g, dil, s // dil)
                  for g, (_, dil) in enumerate(DILATED_GROUPS)]
    cos, sin = _rope_tables(s)
    log_gamma = jnp.log1p(-jnp.exp(ret_decay_logit.astype(F32)))

    x2d = x.reshape(t, d)
    for i in range(DEPTH):
        g = gains[i]
        x2d = _ffn(x2d, g[0], w_gate[i, 0], w_up[i, 0], w_down[i, 0], g[1])
        j = i // 2
        if i % 2 == 0:
            proj = _norm_proj(x2d, g[2], a_w_in[j]).reshape(b, s, ATTN_IN_COLS)
            outs, lses = [], []
            for grp, (_, dil) in enumerate(DILATED_GROUPS):
                o, lse = _attn_group(proj, bias_tiles[grp], grp, dil)
                outs.append(o.reshape(t, D_A))
                lses.append(lse.reshape(t, V7X_LANES))
            x2d = _attn_out(x2d, outs, lses, a_w_out[j], g[3])
        else:
            proj = _norm_proj(x2d, g[2], r_w_in[j]).reshape(b, s, RET_IN_COLS)
            y = _retention(proj, log_gamma[j], cos, sin)
            x2d = _proj_out(x2d, y.reshape(t, D_V), r_w_out[j], g[3])
        x2d = _ffn(x2d, g[4], w_gate[i, 1], w_up[i, 1], w_down[i, 1], g[5])
    return x2d.reshape(b, s, d)
```

```python
import functools

import jax
import jax.numpy as jnp
import numpy as np
from jax import lax
from jax.experimental import pallas as pl
from jax.experimental.pallas import tpu as pltpu

D_MODEL = 1024
DEPTH = 4
NORM_EPS = 1e-6
D_FF = 2816
DILATED_GROUPS = ((128, 1), (512, 4), (2048, 16))
N_GROUPS = len(DILATED_GROUPS)
H_A = 16
DH_A = 64
D_A = H_A * DH_A
GROUP_COLS = 3 * D_A
ATTN_IN_COLS = N_GROUPS * GROUP_COLS
NEG_INF = -1e30
NUM_BUCKETS = 32
REL_MAX_DISTANCE = 1024
H_R = 4
DK_R = D_MODEL // H_R
DV_R = 2 * D_MODEL // H_R
D_V = H_R * DV_R
RET_IN_COLS = 2 * H_R * DK_R + 3 * D_V
RET_CHUNK = 128
ROPE_BASE = 10000.0

V7X_LANES = 128
V7X_VMEM_LIMIT_BYTES = 56 * 1024 * 1024

FFN_ROWS = 512
FFN_COLS = 256
PROJ_ROWS = 512
PROJ_COLS = 1024
ATTN_Q = 128
ATTN_RADIUS = 64
OUT_ROWS = 512
LANE_SLABS = D_MODEL // V7X_LANES

BF16 = jnp.bfloat16
F32 = jnp.float32


def _params(*semantics):
    return pltpu.CompilerParams(dimension_semantics=semantics,
                                vmem_limit_bytes=V7X_VMEM_LIMIT_BYTES)


def _resident(shape, index_map):
    return pl.BlockSpec(shape, index_map, pipeline_mode=pl.Buffered(1))


def _rms(x, g):
    ms = jnp.mean(x * x, axis=-1, keepdims=True)
    return x * lax.rsqrt(ms + NORM_EPS) * g


def _dot(a, b):
    return jnp.dot(a, b, preferred_element_type=F32)


def _ffn_kernel(x_ref, gpre_ref, wg_ref, wu_ref, wd_ref, gpost_ref, o_ref, acc_ref):
    x = x_ref[...]
    h = _rms(x, gpre_ref[...]).astype(BF16)
    for c in range(D_FF // FFN_COLS):
        cols = slice(c * FFN_COLS, (c + 1) * FFN_COLS)
        gate = _dot(h, wg_ref[:, cols])
        up = _dot(h, wu_ref[:, cols])
        act = (gate * jax.nn.sigmoid(gate) * up).astype(BF16)
        part = _dot(act, wd_ref[cols, :])
        if c == 0:
            acc_ref[...] = part
        else:
            acc_ref[...] += part
    o_ref[...] = x + 0.5 * _rms(acc_ref[...], gpost_ref[...])


def _ffn(x2d, g_pre, w_gate, w_up, w_down, g_post):
    t = x2d.shape[0]
    row = lambda i: (i, 0)
    fixed = lambda i: (0, 0)
    return pl.pallas_call(
        _ffn_kernel,
        out_shape=jax.ShapeDtypeStruct((t, D_MODEL), F32),
        grid=(t // FFN_ROWS,),
        in_specs=[
            pl.BlockSpec((FFN_ROWS, D_MODEL), row),
            _resident((1, D_MODEL), fixed),
            _resident((D_MODEL, D_FF), fixed),
            _resident((D_MODEL, D_FF), fixed),
            _resident((D_FF, D_MODEL), fixed),
            _resident((1, D_MODEL), fixed),
        ],
        out_specs=pl.BlockSpec((FFN_ROWS, D_MODEL), row),
        scratch_shapes=[pltpu.VMEM((FFN_ROWS, D_MODEL), F32)],
        compiler_params=_params("parallel"),
        name="ffn",
    )(x2d, g_pre, w_gate, w_up, w_down, g_post)


def _attn_proj_kernel(x_ref, g_ref, w_ref, o0_ref, o1_ref, o2_ref, slab_ref):
    rows = x_ref.shape[1]
    h = _rms(x_ref[0], g_ref[...])
    for j in range(LANE_SLABS):
        slab_ref[j] = h[:, j * V7X_LANES:(j + 1) * V7X_LANES]
    hb = h.astype(BF16)
    for c in range(GROUP_COLS // PROJ_COLS):
        cols = slice(c * PROJ_COLS, (c + 1) * PROJ_COLS)
        o0_ref[0, 0, :, cols] = _dot(hb, w_ref[:, cols]).astype(BF16)
    for grp, o_ref in ((1, o1_ref), (2, o2_ref)):
        dil = DILATED_GROUPS[grp][1]
        n = rows // dil
        by_residue = [
            jnp.concatenate([slab_ref[j, pl.ds(r, n, stride=dil), :] for j in range(LANE_SLABS)], axis=-1)
            for r in range(dil)]
        hp = jnp.concatenate(by_residue, axis=0).astype(BF16)
        for c in range(GROUP_COLS // PROJ_COLS):
            cols = slice(c * PROJ_COLS, (c + 1) * PROJ_COLS)
            wcols = slice(grp * GROUP_COLS + c * PROJ_COLS, grp * GROUP_COLS + (c + 1) * PROJ_COLS)
            res = _dot(hp, w_ref[:, wcols]).astype(BF16)
            for r in range(dil):
                o_ref[0, r, :, cols] = res[r * n:(r + 1) * n, :]


def _attn_proj(x, g, w_in):
    b, s, _ = x.shape
    shapes, specs = [], []
    for _, dil in DILATED_GROUPS:
        shapes.append(jax.ShapeDtypeStruct((b, dil, s // dil, GROUP_COLS), BF16))
        specs.append(pl.BlockSpec((1, dil, PROJ_ROWS // dil, GROUP_COLS), lambda bi, i: (bi, 0, i, 0)))
    return pl.pallas_call(
        _attn_proj_kernel,
        out_shape=tuple(shapes),
        grid=(b, s // PROJ_ROWS),
        in_specs=[
            pl.BlockSpec((1, PROJ_ROWS, D_MODEL), lambda bi, i: (bi, i, 0)),
            _resident((1, D_MODEL), lambda bi, i: (0, 0)),
            _resident((D_MODEL, ATTN_IN_COLS), lambda bi, i: (0, 0)),
        ],
        out_specs=tuple(specs),
        scratch_shapes=[pltpu.VMEM((LANE_SLABS, PROJ_ROWS, V7X_LANES), F32)],
        compiler_params=_params("parallel", "parallel"),
        name="attn_proj",
    )(x, g, w_in)


def _t5_buckets(rel):
    half = NUM_BUCKETS // 2
    max_exact = half // 2
    n = np.abs(rel)
    large = max_exact + (np.log(np.maximum(n, 1) / max_exact)
                         / np.log(REL_MAX_DISTANCE / max_exact) * (half - max_exact)).astype(np.int64)
    large = np.minimum(large, half - 1)
    return ((rel > 0) * half + np.where(n < max_exact, n, large)).astype(np.int32)


def _attn_window(ls):
    return min(ls, ATTN_Q + 2 * ATTN_RADIUS)


def _window_shifts(ls):
    w = _attn_window(ls)
    if w == ls:
        return (0,)
    return (0, -ATTN_RADIUS, ATTN_Q - w)


def _bias_tiles(rel_bias, g, dilation, ls):
    w = _attn_window(ls)
    n = w + ATTN_Q
    buckets = _t5_buckets(np.arange(-ATTN_RADIUS, ATTN_RADIUS + 1) * dilation)
    bias_g = rel_bias[g * H_A:(g + 1) * H_A][:, buckets].astype(F32)
    m = np.arange(n)
    key_minus_query = np.where(m < w, m, m - n)
    tiles = []
    for shift in _window_shifts(ls):
        off = shift + key_minus_query
        valid = np.abs(off) <= ATTN_RADIUS
        idx = np.clip(off + ATTN_RADIUS, 0, 2 * ATTN_RADIUS)
        band = jnp.where(valid[None], bias_g[:, idx], NEG_INF)
        flat = jnp.tile(band, (1, ATTN_Q))[:, :ATTN_Q * (n - 1)]
        tiles.append(flat.reshape(H_A, ATTN_Q, n - 1)[:, :, :w])
    return jnp.stack(tiles, axis=0)


def _attn_kernel(q_ref, k_ref, v_ref, bias_ref, o_ref, lse_ref, *, ls):
    w = _attn_window(ls)
    n_blocks = ls // ATTN_Q
    i = pl.program_id(2)
    if w == ls:
        start = 0
        variant = 0
    else:
        start = pl.multiple_of(jnp.clip(i * ATTN_Q - ATTN_RADIUS, 0, ls - w), ATTN_RADIUS)
        variant = jnp.where(i == 0, 0, jnp.where(i == n_blocks - 1, 2, 1))
    lane = lax.broadcasted_iota(jnp.int32, (ATTN_Q, V7X_LANES), 1)
    lse_tile = jnp.zeros((ATTN_Q, V7X_LANES), F32)
    for pair in range(H_A * DH_A // V7X_LANES):
        cols = slice(pair * V7X_LANES, (pair + 1) * V7X_LANES)
        qp = q_ref[0, 0, :, cols] * jnp.asarray(DH_A ** -0.5, BF16)
        kp = k_ref[0, 0, pl.ds(start, w), cols]
        vp = v_ref[0, 0, pl.ds(start, w), cols]
        out = jnp.zeros((ATTN_Q, V7X_LANES), F32)
        for sub in range(V7X_LANES // DH_A):
            head = pair * (V7X_LANES // DH_A) + sub
            mine = (lane >= sub * DH_A) & (lane < (sub + 1) * DH_A)
            qh = jnp.where(mine, qp, jnp.zeros_like(qp))
            s = lax.dot_general(qh, kp, (((1,), (1,)), ((), ())), preferred_element_type=F32)
            s = s + bias_ref[variant, head]
            m = jnp.max(s, axis=-1, keepdims=True)
            e = jnp.exp(s - m)
            denom = jnp.sum(e, axis=-1, keepdims=True)
            pv = _dot(e.astype(BF16), vp)
            out = jnp.where(mine, pv / denom, out)
            lse_tile = jnp.where(lane == head, m + jnp.log(denom), lse_tile)
        o_ref[0, 0, :, cols] = out.astype(BF16)
    lse_ref[0, 0] = lse_tile


def _attn_group(qkv, bias_tiles):
    b, dil, ls, _ = qkv.shape
    w = _attn_window(ls)
    nvar = bias_tiles.shape[0]
    return pl.pallas_call(
        functools.partial(_attn_kernel, ls=ls),
        out_shape=(jax.ShapeDtypeStruct((b, dil, ls, D_A), BF16),
                   jax.ShapeDtypeStruct((b, dil, ls, V7X_LANES), F32)),
        grid=(b, dil, ls // ATTN_Q),
        in_specs=[
            pl.BlockSpec((1, 1, ATTN_Q, D_A), lambda bi, r, i: (bi, r, i, 0)),
            pl.BlockSpec((1, 1, ls, D_A), lambda bi, r, i: (bi, r, 0, 1)),
            pl.BlockSpec((1, 1, ls, D_A), lambda bi, r, i: (bi, r, 0, 2)),
            _resident((nvar, H_A, ATTN_Q, w), lambda bi, r, i: (0, 0, 0, 0)),
        ],
        out_specs=(pl.BlockSpec((1, 1, ATTN_Q, D_A), lambda bi, r, i: (bi, r, i, 0)),
                   pl.BlockSpec((1, 1, ATTN_Q, V7X_LANES), lambda bi, r, i: (bi, r, i, 0))),
        compiler_params=_params("parallel", "parallel", "arbitrary"),
        name=f"attn_group_d{dil}",
    )(qkv, qkv, qkv, bias_tiles)


def _attn_out_kernel(x_ref, o0_ref, o1_ref, o2_ref, l0_ref, l1_ref, l2_ref,
                     expand_ref, w_ref, g_ref, out_ref, onat_ref, lnat_ref):
    rows = x_ref.shape[1]
    for slot, (o_ref, l_ref) in enumerate(((o1_ref, l1_ref), (o2_ref, l2_ref))):
        dil = o_ref.shape[1]
        n = rows // dil
        for r in range(dil):
            lnat_ref[slot, pl.ds(r, n, stride=dil), :] = l_ref[0, r]
            blk = o_ref[0, r].astype(F32)
            for j in range(LANE_SLABS):
                onat_ref[slot, j, pl.ds(r, n, stride=dil), :] = blk[:, j * V7X_LANES:(j + 1) * V7X_LANES]
    lses = [l0_ref[0, 0], lnat_ref[0], lnat_ref[1]]
    outs = [o0_ref[0, 0].astype(F32)] + [
        jnp.concatenate([onat_ref[slot, j] for j in range(LANE_SLABS)], axis=-1) for slot in range(2)]
    m = jnp.maximum(jnp.maximum(lses[0], lses[1]), lses[2])
    es = [jnp.exp(l - m) for l in lses]
    denom = es[0] + es[1] + es[2]
    merged = None
    for e, o in zip(es, outs):
        wt = (e / denom).astype(BF16)
        wt_wide = _dot(wt, expand_ref[...])
        term = wt_wide * o
        merged = term if merged is None else merged + term
    y = _dot(merged.astype(BF16), w_ref[...])
    out_ref[0] = x_ref[0] + _rms(y, g_ref[...])


def _attn_out(x, outs, lses, w_out, g):
    b, s, _ = x.shape
    head_of_col = np.arange(D_A) // DH_A
    expand = jnp.asarray(np.arange(V7X_LANES)[:, None] == head_of_col[None, :], BF16)
    tok = lambda bi, i: (bi, i, 0)
    grp = lambda bi, i: (bi, 0, i, 0)
    fixed = lambda bi, i: (0, 0)
    dils = [dil for _, dil in DILATED_GROUPS]
    return pl.pallas_call(
        _attn_out_kernel,
        out_shape=jax.ShapeDtypeStruct((b, s, D_MODEL), F32),
        grid=(b, s // OUT_ROWS),
        in_specs=[pl.BlockSpec((1, OUT_ROWS, D_MODEL), tok)]
        + [pl.BlockSpec((1, dil, OUT_ROWS // dil, D_A), grp) for dil in dils]
        + [pl.BlockSpec((1, dil, OUT_ROWS // dil, V7X_LANES), grp) for dil in dils]
        + [_resident((V7X_LANES, D_A), fixed),
           _resident((D_A, D_MODEL), fixed),
           _resident((1, D_MODEL), fixed)],
        out_specs=pl.BlockSpec((1, OUT_ROWS, D_MODEL), tok),
        scratch_shapes=[pltpu.VMEM((2, LANE_SLABS, OUT_ROWS, V7X_LANES), F32),
                        pltpu.VMEM((2, OUT_ROWS, V7X_LANES), F32)],
        compiler_params=_params("parallel", "parallel"),
        name="attn_out",
    )(x, *outs, *lses, expand, w_out, g)


def _norm_proj_kernel(x_ref, g_ref, w_ref, o_ref):
    h = _rms(x_ref[...], g_ref[...]).astype(BF16)
    n = w_ref.shape[1]
    for c in range(n // PROJ_COLS):
        cols = slice(c * PROJ_COLS, (c + 1) * PROJ_COLS)
        o_ref[:, cols] = _dot(h, w_ref[:, cols]).astype(BF16)


def _norm_proj(x2d, g, w_in):
    t = x2d.shape[0]
    n = w_in.shape[1]
    row = lambda i: (i, 0)
    fixed = lambda i: (0, 0)
    return pl.pallas_call(
        _norm_proj_kernel,
        out_shape=jax.ShapeDtypeStruct((t, n), BF16),
        grid=(t // PROJ_ROWS,),
        in_specs=[
            pl.BlockSpec((PROJ_ROWS, D_MODEL), row),
            _resident((1, D_MODEL), fixed),
            _resident((D_MODEL, n), fixed),
        ],
        out_specs=pl.BlockSpec((PROJ_ROWS, n), row),
        compiler_params=_params("parallel"),
        name="norm_proj",
    )(x2d, g, w_in)


def _rotate(t_ref, rows, cos, sin, scale):
    half = DK_R // 2
    t1 = t_ref[0, rows, :half].astype(F32)
    t2 = t_ref[0, rows, half:].astype(F32)
    out = jnp.concatenate([t1 * cos - t2 * sin, t1 * sin + t2 * cos], axis=-1)
    if scale != 1.0:
        out = out * scale
    return out


def _group_norm(y):
    mu = jnp.mean(y, axis=-1, keepdims=True)
    d = y - mu
    var = jnp.mean(d * d, axis=-1, keepdims=True)
    return d * lax.rsqrt(var + NORM_EPS)


def _retention_kernel(lg_ref, q_ref, k_ref, v_ref, gf_ref, gb_ref, cos_ref, sin_ref,
                      o_ref, state_ref, yf_ref):
    c_len = RET_CHUNK
    s_len = q_ref.shape[1]
    n_chunks = s_len // c_len
    head = pl.program_id(1)
    ii = lax.broadcasted_iota(jnp.int32, (c_len, c_len), 0).astype(F32)
    jj = lax.broadcasted_iota(jnp.int32, (c_len, c_len), 1).astype(F32)
    col = lax.broadcasted_iota(jnp.int32, (c_len, 1), 0).astype(F32)

    def scan(direction):
        lg = lg_ref[direction, head]
        if direction == 0:
            rel = ii - jj
            q_pow = col + 1.0
            k_pow = (c_len - 1.0) - col
        else:
            rel = jj - ii
            q_pow = c_len - col
            k_pow = col
        dmat = jnp.where(rel >= 0, jnp.exp(lg * jnp.maximum(rel, 0.0)), 0.0)
        q_decay = jnp.exp(lg * q_pow)
        k_decay = jnp.exp(lg * k_pow)
        chunk_decay = jnp.exp(lg * c_len)
        gate_ref = gf_ref if direction == 0 else gb_ref
        state_ref[...] = jnp.zeros_like(state_ref)

        def body(n, carry):
            chunk = n if direction == 0 else n_chunks - 1 - n
            rows = pl.ds(pl.multiple_of(chunk * c_len, c_len), c_len)
            cos = cos_ref[rows, :]
            sin = sin_ref[rows, :]
            q = _rotate(q_ref, rows, cos, sin, 1.0)
            k = _rotate(k_ref, rows, cos, sin, DK_R ** -0.5)
            v = v_ref[0, rows, :]
            qb = q.astype(BF16)
            scores = lax.dot_general(qb, k.astype(BF16), (((1,), (1,)), ((), ())),
                                     preferred_element_type=F32) * dmat
            inner = _dot(scores.astype(BF16), v)
            state = state_ref[...]
            cross = _dot(qb, state.astype(BF16)) * q_decay
            kd = (k * k_decay).astype(BF16)
            state_ref[...] = state * chunk_decay + lax.dot_general(
                kd, v, (((0,), (0,)), ((), ())), preferred_element_type=F32)
            gate = gate_ref[0, rows, :].astype(F32)
            y = gate * jax.nn.sigmoid(gate) * _group_norm(inner + cross)
            if direction == 0:
                yf_ref[rows, :] = y
            else:
                o_ref[0, rows, :] = (yf_ref[rows, :] + y).astype(BF16)
            return carry

        lax.fori_loop(0, n_chunks, body, 0)

    scan(0)
    scan(1)


def _retention(proj, log_gamma, cos, sin):
    b, s, _ = proj.shape
    q0 = 0
    k0 = H_R
    v0 = (2 * H_R * DK_R) // DV_R
    gf0 = v0 + H_R
    gb0 = gf0 + H_R
    return pl.pallas_call(
        _retention_kernel,
        out_shape=jax.ShapeDtypeStruct((b, s, D_V), BF16),
        grid_spec=pltpu.PrefetchScalarGridSpec(
            num_scalar_prefetch=1,
            grid=(b, H_R),
            in_specs=[
                pl.BlockSpec((1, s, DK_R), lambda bi, h, lg: (bi, 0, q0 + h)),
                pl.BlockSpec((1, s, DK_R), lambda bi, h, lg: (bi, 0, k0 + h)),
                pl.BlockSpec((1, s, DV_R), lambda bi, h, lg: (bi, 0, v0 + h)),
                pl.BlockSpec((1, s, DV_R), lambda bi, h, lg: (bi, 0, gf0 + h)),
                pl.BlockSpec((1, s, DV_R), lambda bi, h, lg: (bi, 0, gb0 + h)),
                _resident((s, DK_R // 2), lambda bi, h, lg: (0, 0)),
                _resident((s, DK_R // 2), lambda bi, h, lg: (0, 0)),
            ],
            out_specs=pl.BlockSpec((1, s, DV_R), lambda bi, h, lg: (bi, 0, h)),
            scratch_shapes=[pltpu.VMEM((DK_R, DV_R), F32), pltpu.VMEM((s, DV_R), F32)],
        ),
        compiler_params=_params("parallel", "parallel"),
        name="retention",
    )(log_gamma, proj, proj, proj, proj, proj, cos, sin)


def _proj_out_kernel(x_ref, y_ref, w_ref, g_ref, out_ref):
    out_ref[...] = x_ref[...] + _rms(_dot(y_ref[...], w_ref[...]), g_ref[...])


def _proj_out(x2d, y2d, w_out, g):
    t = x2d.shape[0]
    k = y2d.shape[1]
    row = lambda i: (i, 0)
    fixed = lambda i: (0, 0)
    return pl.pallas_call(
        _proj_out_kernel,
        out_shape=jax.ShapeDtypeStruct((t, D_MODEL), F32),
        grid=(t // OUT_ROWS,),
        in_specs=[
            pl.BlockSpec((OUT_ROWS, D_MODEL), row),
            pl.BlockSpec((OUT_ROWS, k), row),
            _resident((k, D_MODEL), fixed),
            _resident((1, D_MODEL), fixed),
        ],
        out_specs=pl.BlockSpec((OUT_ROWS, D_MODEL), row),
        compiler_params=_params("parallel"),
        name="proj_out",
    )(x2d, y2d, w_out, g)


def _rope_tables(s):
    half = DK_R // 2
    inv_freq = 1.0 / (ROPE_BASE ** jnp.linspace(0.0, 1.0, half, dtype=F32))
    ang = jnp.arange(s, dtype=F32)[:, None] * inv_freq[None, :]
    return jnp.cos(ang), jnp.sin(ang)


def kernel(x, norm_gains, ffn_w_gate, ffn_w_up, ffn_w_down, attn_w_in, attn_w_out,
           rel_bias, ret_w_in, ret_w_out, ret_decay_logit):
    b, s, d = x.shape
    t = b * s
    gains = norm_gains.astype(F32).reshape(DEPTH, -1, 1, D_MODEL)
    w_gate = ffn_w_gate.astype(BF16)
    w_up = ffn_w_up.astype(BF16)
    w_down = ffn_w_down.astype(BF16)
    a_w_in = attn_w_in.astype(BF16)
    a_w_out = attn_w_out.astype(BF16)
    r_w_in = ret_w_in.astype(BF16)
    r_w_out = ret_w_out.astype(BF16)
    bias_tiles = [_bias_tiles(rel_bias, g, dil, s // dil)
                  for g, (_, dil) in enumerate(DILATED_GROUPS)]
    cos, sin = _rope_tables(s)
    log_gamma = jnp.log1p(-jnp.exp(ret_decay_logit.astype(F32)))

    x2d = x.reshape(t, d)
    for i in range(DEPTH):
        g = gains[i]
        x2d = _ffn(x2d, g[0], w_gate[i, 0], w_up[i, 0], w_down[i, 0], g[1])
        j = i // 2
        if i % 2 == 0:
            x3d = x2d.reshape(b, s, d)
            qkvs = _attn_proj(x3d, g[2], a_w_in[j])
            outs, lses = zip(*[_attn_group(qkv, bt) for qkv, bt in zip(qkvs, bias_tiles)])
            x2d = _attn_out(x3d, outs, lses, a_w_out[j], g[3]).reshape(t, d)
        else:
            proj = _norm_proj(x2d, g[2], r_w_in[j]).reshape(b, s, RET_IN_COLS)
            y = _retention(proj, log_gamma[j], cos, sin)
            x2d = _proj_out(x2d, y.reshape(t, D_V), r_w_out[j], g[3])
        x2d = _ffn(x2d, g[4], w_gate[i, 1], w_up[i, 1], w_down[i, 1], g[5])
    return x2d.reshape(b, s, d)
```

```python
import functools

import jax
import jax.numpy as jnp
import numpy as np
from jax import lax
from jax.experimental import pallas as pl
from jax.experimental.pallas import tpu as pltpu

D_MODEL = 1024
DEPTH = 4
NORM_EPS = 1e-6
D_FF = 2816
DILATED_GROUPS = ((128, 1), (512, 4), (2048, 16))
N_GROUPS = len(DILATED_GROUPS)
H_A = 16
DH_A = 64
D_A = H_A * DH_A
GROUP_COLS = 3 * D_A
ATTN_IN_COLS = N_GROUPS * GROUP_COLS
NEG_INF = -1e30
NUM_BUCKETS = 32
REL_MAX_DISTANCE = 1024
H_R = 4
DK_R = D_MODEL // H_R
DV_R = 2 * D_MODEL // H_R
D_V = H_R * DV_R
RET_QK_COLS = H_R * DK_R
RET_IN_COLS = 2 * RET_QK_COLS + 3 * D_V
RET_CHUNK = 128
ROPE_BASE = 10000.0

V7X_LANES = 128
V7X_VMEM_LIMIT_BYTES = 56 * 1024 * 1024

FFN_ROWS = 512
FFN_COLS = 256
PROJ_ROWS = 512
PROJ_COLS = 1024
ATTN_Q = 128
ATTN_RADIUS = 64
OUT_ROWS = 512
RET_UNROLL = 2
LANE_SLABS = D_MODEL // V7X_LANES
HEAD_PAIRS = D_A // V7X_LANES

BF16 = jnp.bfloat16
F32 = jnp.float32


def _params(*semantics):
    return pltpu.CompilerParams(dimension_semantics=semantics,
                                vmem_limit_bytes=V7X_VMEM_LIMIT_BYTES)


def _resident(shape, index_map):
    return pl.BlockSpec(shape, index_map, pipeline_mode=pl.Buffered(1))


def _rms(x, g):
    ms = jnp.mean(x * x, axis=-1, keepdims=True)
    return x * lax.rsqrt(ms + NORM_EPS) * g


def _dot(a, b):
    return jnp.dot(a, b, preferred_element_type=F32)


def _dot_nt(a, b):
    return lax.dot_general(a, b, (((1,), (1,)), ((), ())), preferred_element_type=F32)


def _dot_tn(a, b):
    return lax.dot_general(a, b, (((0,), (0,)), ((), ())), preferred_element_type=F32)


def _ffn_kernel(x_ref, gpre_ref, wg_ref, wu_ref, wd_ref, gpost_ref, o_ref, acc_ref):
    x = x_ref[...]
    h = _rms(x, gpre_ref[...]).astype(BF16)
    for c in range(D_FF // FFN_COLS):
        cols = slice(c * FFN_COLS, (c + 1) * FFN_COLS)
        gate = _dot(h, wg_ref[:, cols])
        up = _dot(h, wu_ref[:, cols])
        act = (gate * jax.nn.sigmoid(gate) * up).astype(BF16)
        part = _dot(act, wd_ref[cols, :])
        if c == 0:
            acc_ref[...] = part
        else:
            acc_ref[...] += part
    o_ref[...] = x + 0.5 * _rms(acc_ref[...], gpost_ref[...])


def _ffn(x2d, g_pre, w_gate, w_up, w_down, g_post):
    t = x2d.shape[0]
    row = lambda i: (i, 0)
    fixed = lambda i: (0, 0)
    return pl.pallas_call(
        _ffn_kernel,
        out_shape=jax.ShapeDtypeStruct((t, D_MODEL), F32),
        grid=(t // FFN_ROWS,),
        in_specs=[
            pl.BlockSpec((FFN_ROWS, D_MODEL), row),
            _resident((1, D_MODEL), fixed),
            _resident((D_MODEL, D_FF), fixed),
            _resident((D_MODEL, D_FF), fixed),
            _resident((D_FF, D_MODEL), fixed),
            _resident((1, D_MODEL), fixed),
        ],
        out_specs=pl.BlockSpec((FFN_ROWS, D_MODEL), row),
        scratch_shapes=[pltpu.VMEM((FFN_ROWS, D_MODEL), F32)],
        compiler_params=_params("parallel"),
        name="ffn",
    )(x2d, g_pre, w_gate, w_up, w_down, g_post)


def _qkv_chunk(h, w_ref, grp, c):
    first = grp * GROUP_COLS + c * PROJ_COLS
    res = _dot(h, w_ref[:, first:first + PROJ_COLS])
    if c * PROJ_COLS < D_A:
        res = res * (DH_A ** -0.5)
    return res.astype(BF16)


def _attn_proj_kernel(x_ref, g_ref, w_ref, o0_ref, o1_ref, o2_ref, slab_ref):
    rows = x_ref.shape[1]
    h = _rms(x_ref[0], g_ref[...])
    for j in range(LANE_SLABS):
        slab_ref[j] = h[:, j * V7X_LANES:(j + 1) * V7X_LANES]
    hb = h.astype(BF16)
    for c in range(GROUP_COLS // PROJ_COLS):
        cols = slice(c * PROJ_COLS, (c + 1) * PROJ_COLS)
        o0_ref[0, 0, :, cols] = _qkv_chunk(hb, w_ref, 0, c)
    for grp, o_ref in ((1, o1_ref), (2, o2_ref)):
        dil = DILATED_GROUPS[grp][1]
        n = rows // dil
        by_residue = [
            jnp.concatenate([slab_ref[j, pl.ds(r, n, stride=dil), :] for j in range(LANE_SLABS)], axis=-1)
            for r in range(dil)]
        hp = jnp.concatenate(by_residue, axis=0).astype(BF16)
        for c in range(GROUP_COLS // PROJ_COLS):
            cols = slice(c * PROJ_COLS, (c + 1) * PROJ_COLS)
            res = _qkv_chunk(hp, w_ref, grp, c)
            for r in range(dil):
                o_ref[0, r, :, cols] = res[r * n:(r + 1) * n, :]


def _attn_proj(x, g, w_in):
    b, s, _ = x.shape
    shapes, specs = [], []
    for _, dil in DILATED_GROUPS:
        shapes.append(jax.ShapeDtypeStruct((b, dil, s // dil, GROUP_COLS), BF16))
        specs.append(pl.BlockSpec((1, dil, PROJ_ROWS // dil, GROUP_COLS), lambda bi, i: (bi, 0, i, 0)))
    return pl.pallas_call(
        _attn_proj_kernel,
        out_shape=tuple(shapes),
        grid=(b, s // PROJ_ROWS),
        in_specs=[
            pl.BlockSpec((1, PROJ_ROWS, D_MODEL), lambda bi, i: (bi, i, 0)),
            _resident((1, D_MODEL), lambda bi, i: (0, 0)),
            _resident((D_MODEL, ATTN_IN_COLS), lambda bi, i: (0, 0)),
        ],
        out_specs=tuple(specs),
        scratch_shapes=[pltpu.VMEM((LANE_SLABS, PROJ_ROWS, V7X_LANES), F32)],
        compiler_params=_params("parallel", "parallel"),
        name="attn_proj",
    )(x, g, w_in)


def _t5_buckets(rel):
    half = NUM_BUCKETS // 2
    max_exact = half // 2
    n = np.abs(rel)
    large = max_exact + (np.log(np.maximum(n, 1) / max_exact)
                         / np.log(REL_MAX_DISTANCE / max_exact) * (half - max_exact)).astype(np.int64)
    large = np.minimum(large, half - 1)
    return ((rel > 0) * half + np.where(n < max_exact, n, large)).astype(np.int32)


def _attn_window(ls):
    return min(ls, ATTN_Q + 2 * ATTN_RADIUS)


def _window_shifts(ls):
    w = _attn_window(ls)
    if w == ls:
        return (0,)
    return (0, -ATTN_RADIUS, ATTN_Q - w)


def _bias_tiles(rel_bias, g, dilation, ls):
    w = _attn_window(ls)
    n = w + ATTN_Q
    buckets = _t5_buckets(np.arange(-ATTN_RADIUS, ATTN_RADIUS + 1) * dilation)
    bias_g = rel_bias[g * H_A:(g + 1) * H_A][:, buckets].astype(F32)
    m = np.arange(n)
    key_minus_query = np.where(m < w, m, m - n)
    tiles = []
    for shift in _window_shifts(ls):
        off = shift + key_minus_query
        valid = np.abs(off) <= ATTN_RADIUS
        idx = np.clip(off + ATTN_RADIUS, 0, 2 * ATTN_RADIUS)
        band = jnp.where(valid[None], bias_g[:, idx], NEG_INF)
        flat = jnp.tile(band, (1, ATTN_Q))[:, :ATTN_Q * (n - 1)]
        tiles.append(flat.reshape(H_A, ATTN_Q, n - 1)[:, :, :w])
    return jnp.stack(tiles, axis=0)


def _stat_lane(pair, sub):
    return DH_A + pair if sub == 0 else pair


def _attn_kernel(q_ref, k_ref, v_ref, bias_ref, o_ref, m_ref, l_ref, *, ls):
    w = _attn_window(ls)
    n_blocks = ls // ATTN_Q
    i = pl.program_id(2)
    if w == ls:
        start = 0
        variant = 0
    else:
        start = pl.multiple_of(jnp.clip(i * ATTN_Q - ATTN_RADIUS, 0, ls - w), ATTN_RADIUS)
        variant = jnp.where(i == 0, 0, jnp.where(i == n_blocks - 1, 2, 1))
    subs = V7X_LANES // DH_A
    heads = [(pair, sub) for pair in range(HEAD_PAIRS) for sub in range(subs)]
    q_lane = lax.broadcasted_iota(jnp.int32, (ATTN_Q, V7X_LANES), 1)
    v_lane = lax.broadcasted_iota(jnp.int32, (w, V7X_LANES), 1)
    q_half = [q_lane < DH_A, q_lane >= DH_A]
    v_half = [v_lane < DH_A, v_lane >= DH_A]

    scores = {}
    for pair, sub in heads:
        cols = slice(pair * V7X_LANES, (pair + 1) * V7X_LANES)
        qp = q_ref[0, 0, :, cols]
        qh = jnp.where(q_half[sub], qp, jnp.zeros_like(qp))
        kp = k_ref[0, 0, pl.ds(start, w), cols]
        scores[pair, sub] = _dot_nt(qh, kp) + bias_ref[variant, pair * subs + sub]
    maxima, probs = {}, {}
    for head, s in scores.items():
        maxima[head] = jnp.max(s, axis=-1, keepdims=True)
        probs[head] = jnp.exp(s - maxima[head]).astype(BF16)
    m_tile = jnp.zeros((ATTN_Q, V7X_LANES), F32)
    l_tile = jnp.ones((ATTN_Q, V7X_LANES), F32)
    for pair in range(HEAD_PAIRS):
        cols = slice(pair * V7X_LANES, (pair + 1) * V7X_LANES)
        vp = v_ref[0, 0, pl.ds(start, w), cols]
        out = None
        for sub in range(subs):
            vh = jnp.where(v_half[sub], vp, jnp.ones_like(vp))
            pv = _dot(probs[pair, sub], vh)
            here = q_lane == _stat_lane(pair, sub)
            l_tile = jnp.where(here, pv, l_tile)
            m_tile = jnp.where(here, maxima[pair, sub], m_tile)
            out = pv if out is None else jnp.where(q_half[sub], pv, out)
        o_ref[0, 0, :, cols] = out.astype(BF16)
    m_ref[0, 0] = m_tile
    l_ref[0, 0] = l_tile


def _attn_group(qkv, bias_tiles):
    b, dil, ls, _ = qkv.shape
    w = _attn_window(ls)
    nvar = bias_tiles.shape[0]
    blk = lambda bi, r, i: (bi, r, i, 0)
    return pl.pallas_call(
        functools.partial(_attn_kernel, ls=ls),
        out_shape=(jax.ShapeDtypeStruct((b, dil, ls, D_A), BF16),
                   jax.ShapeDtypeStruct((b, dil, ls, V7X_LANES), F32),
                   jax.ShapeDtypeStruct((b, dil, ls, V7X_LANES), F32)),
        grid=(b, dil, ls // ATTN_Q),
        in_specs=[
            pl.BlockSpec((1, 1, ATTN_Q, D_A), blk),
            pl.BlockSpec((1, 1, ls, D_A), lambda bi, r, i: (bi, r, 0, 1)),
            pl.BlockSpec((1, 1, ls, D_A), lambda bi, r, i: (bi, r, 0, 2)),
            _resident((nvar, H_A, ATTN_Q, w), lambda bi, r, i: (0, 0, 0, 0)),
        ],
        out_specs=(pl.BlockSpec((1, 1, ATTN_Q, D_A), blk),
                   pl.BlockSpec((1, 1, ATTN_Q, V7X_LANES), blk),
                   pl.BlockSpec((1, 1, ATTN_Q, V7X_LANES), blk)),
        compiler_params=_params("parallel", "parallel", "arbitrary"),
        name=f"attn_group_d{dil}",
    )(qkv, qkv, qkv, bias_tiles)


def _attn_out_kernel(x_ref, o0_ref, o1_ref, o2_ref, m0_ref, m1_ref, m2_ref, l0_ref, l1_ref, l2_ref,
                     expand_ref, w_ref, g_ref, out_ref, onat_ref, snat_ref):
    rows = x_ref.shape[1]
    for slot, (o_ref, m_ref, l_ref) in enumerate(((o1_ref, m1_ref, l1_ref), (o2_ref, m2_ref, l2_ref))):
        dil = o_ref.shape[1]
        n = rows // dil
        for r in range(dil):
            token_rows = pl.ds(r, n, stride=dil)
            snat_ref[slot, 0, token_rows, :] = m_ref[0, r]
            snat_ref[slot, 1, token_rows, :] = l_ref[0, r]
            blk = o_ref[0, r].astype(F32)
            for j in range(LANE_SLABS):
                onat_ref[slot, j, token_rows, :] = blk[:, j * V7X_LANES:(j + 1) * V7X_LANES]
    maxima = [m0_ref[0, 0], snat_ref[0, 0], snat_ref[1, 0]]
    sums = [l0_ref[0, 0], snat_ref[0, 1], snat_ref[1, 1]]
    outs = [o0_ref[0, 0].astype(F32)] + [
        jnp.concatenate([onat_ref[slot, j] for j in range(LANE_SLABS)], axis=-1) for slot in range(2)]
    top = jnp.maximum(jnp.maximum(maxima[0], maxima[1]), maxima[2])
    scale = [jnp.exp(m - top) for m in maxima]
    denom = scale[0] * sums[0] + scale[1] * sums[1] + scale[2] * sums[2]
    merged = None
    for sc, o in zip(scale, outs):
        wt = (sc / denom).astype(BF16)
        wt_wide = _dot(wt, expand_ref[...])
        term = wt_wide * o
        merged = term if merged is None else merged + term
    y = _dot(merged.astype(BF16), w_ref[...])
    out_ref[0] = x_ref[0] + _rms(y, g_ref[...])


def _attn_out(x, outs, maxima, sums, w_out, g):
    b, s, _ = x.shape
    head_of_col = np.arange(D_A) // DH_A
    lane_of_col = np.array([_stat_lane(h // 2, h % 2) for h in head_of_col])
    expand = jnp.asarray(np.arange(V7X_LANES)[:, None] == lane_of_col[None, :], BF16)
    tok = lambda bi, i: (bi, i, 0)
    grp = lambda bi, i: (bi, 0, i, 0)
    fixed = lambda bi, i: (0, 0)
    dils = [dil for _, dil in DILATED_GROUPS]
    stat_specs = [pl.BlockSpec((1, dil, OUT_ROWS // dil, V7X_LANES), grp) for dil in dils]
    return pl.pallas_call(
        _attn_out_kernel,
        out_shape=jax.ShapeDtypeStruct((b, s, D_MODEL), F32),
        grid=(b, s // OUT_ROWS),
        in_specs=[pl.BlockSpec((1, OUT_ROWS, D_MODEL), tok)]
        + [pl.BlockSpec((1, dil, OUT_ROWS // dil, D_A), grp) for dil in dils]
        + stat_specs + stat_specs
        + [_resident((V7X_LANES, D_A), fixed),
           _resident((D_A, D_MODEL), fixed),
           _resident((1, D_MODEL), fixed)],
        out_specs=pl.BlockSpec((1, OUT_ROWS, D_MODEL), tok),
        scratch_shapes=[pltpu.VMEM((2, LANE_SLABS, OUT_ROWS, V7X_LANES), F32),
                        pltpu.VMEM((2, 2, OUT_ROWS, V7X_LANES), F32)],
        compiler_params=_params("parallel", "parallel"),
        name="attn_out",
    )(x, *outs, *maxima, *sums, expand, w_out, g)


def _rotary(t, cos, sin):
    half = DK_R // 2
    parts = []
    for h in range(t.shape[1] // DK_R):
        t1 = t[:, h * DK_R:h * DK_R + half]
        t2 = t[:, h * DK_R + half:(h + 1) * DK_R]
        parts += [t1 * cos - t2 * sin, t1 * sin + t2 * cos]
    return jnp.concatenate(parts, axis=-1)


def _ret_proj_kernel(x_ref, g_ref, w_ref, cos_ref, sin_ref, o_ref):
    h = _rms(x_ref[...], g_ref[...]).astype(BF16)
    for c in range(RET_IN_COLS // PROJ_COLS):
        first = c * PROJ_COLS
        cols = slice(first, first + PROJ_COLS)
        res = _dot(h, w_ref[:, cols])
        if first < 2 * RET_QK_COLS:
            res = _rotary(res, cos_ref[...], sin_ref[...])
            if first >= RET_QK_COLS:
                res = res * (DK_R ** -0.5)
        o_ref[:, cols] = res.astype(BF16)


def _ret_proj(x2d, g, w_in, cos, sin, s):
    t = x2d.shape[0]
    assert RET_QK_COLS % PROJ_COLS == 0 and PROJ_COLS % DK_R == 0
    row = lambda i: (i, 0)
    fixed = lambda i: (0, 0)
    pos = lambda i: (i % (s // PROJ_ROWS), 0)
    return pl.pallas_call(
        _ret_proj_kernel,
        out_shape=jax.ShapeDtypeStruct((t, RET_IN_COLS), BF16),
        grid=(t // PROJ_ROWS,),
        in_specs=[
            pl.BlockSpec((PROJ_ROWS, D_MODEL), row),
            _resident((1, D_MODEL), fixed),
            _resident((D_MODEL, RET_IN_COLS), fixed),
            pl.BlockSpec((PROJ_ROWS, DK_R // 2), pos),
            pl.BlockSpec((PROJ_ROWS, DK_R // 2), pos),
        ],
        out_specs=pl.BlockSpec((PROJ_ROWS, RET_IN_COLS), row),
        compiler_params=_params("parallel"),
        name="ret_proj",
    )(x2d, g, w_in, cos, sin)


def _group_norm(y):
    mu = jnp.mean(y, axis=-1, keepdims=True)
    d = y - mu
    var = jnp.mean(d * d, axis=-1, keepdims=True)
    return d * lax.rsqrt(var + NORM_EPS)


def _retention_kernel(lg_ref, q_ref, k_ref, v_ref, gf_ref, gb_ref, o_ref, state_ref, y_ref):
    c_len = RET_CHUNK
    n_chunks = q_ref.shape[1] // c_len
    head = pl.program_id(1)
    ii = lax.broadcasted_iota(jnp.int32, (c_len, c_len), 0).astype(F32)
    jj = lax.broadcasted_iota(jnp.int32, (c_len, c_len), 1).astype(F32)
    col = lax.broadcasted_iota(jnp.int32, (c_len, 1), 0).astype(F32)
    dirs = (0, 1)
    gate_refs = (gf_ref, gb_ref)
    dmat, q_decay, k_decay, chunk_decay = [], [], [], []
    for d in dirs:
        lg = lg_ref[d, head]
        rel = ii - jj if d == 0 else jj - ii
        dmat.append(jnp.where(rel >= 0, jnp.exp(lg * jnp.maximum(rel, 0.0)), 0.0))
        q_decay.append(jnp.exp(lg * (col + 1.0 if d == 0 else c_len - col)))
        k_decay.append(jnp.exp(lg * (c_len - 1.0 - col if d == 0 else col)))
        chunk_decay.append(jnp.exp(lg * c_len))
    state_ref[...] = jnp.zeros_like(state_ref)

    def step(n, finish):
        rows = [pl.ds(pl.multiple_of(chunk * c_len, c_len), c_len) for chunk in (n, n_chunks - 1 - n)]
        q = [q_ref[0, rows[d], :] for d in dirs]
        k = [k_ref[0, rows[d], :] for d in dirs]
        v = [v_ref[0, rows[d], :] for d in dirs]
        state = [state_ref[d] for d in dirs]
        scores = [_dot_nt(q[d], k[d]) for d in dirs]
        cross = [_dot(q[d], state[d].astype(BF16)) for d in dirs]
        probs = [(scores[d] * dmat[d]).astype(BF16) for d in dirs]
        kd = [(k[d].astype(F32) * k_decay[d]).astype(BF16) for d in dirs]
        inner = [_dot(probs[d], v[d]) for d in dirs]
        update = [_dot_tn(kd[d], v[d]) for d in dirs]
        for d in dirs:
            state_ref[d] = state[d] * chunk_decay[d] + update[d]
        for d in dirs:
            gate = gate_refs[d][0, rows[d], :].astype(F32)
            y = gate * jax.nn.sigmoid(gate) * _group_norm(inner[d] + cross[d] * q_decay[d])
            if finish:
                o_ref[0, rows[d], :] = (y + y_ref[1 - d, rows[d], :]).astype(BF16)
            else:
                y_ref[d, rows[d], :] = y

    def first_half(n, carry):
        step(n, False)
        return carry

    def second_half(n, carry):
        step(n, True)
        return carry

    lax.fori_loop(0, n_chunks // 2, first_half, 0, unroll=RET_UNROLL)
    lax.fori_loop(n_chunks // 2, n_chunks, second_half, 0, unroll=RET_UNROLL)


def _retention(proj, log_gamma):
    b, s, _ = proj.shape
    assert (s // RET_CHUNK) % 2 == 0
    q0 = 0
    k0 = H_R
    v0 = (2 * RET_QK_COLS) // DV_R
    gf0 = v0 + H_R
    gb0 = gf0 + H_R
    return pl.pallas_call(
        _retention_kernel,
        out_shape=jax.ShapeDtypeStruct((b, s, D_V), BF16),
        grid_spec=pltpu.PrefetchScalarGridSpec(
            num_scalar_prefetch=1,
            grid=(b, H_R),
            in_specs=[
                pl.BlockSpec((1, s, DK_R), lambda bi, h, lg: (bi, 0, q0 + h)),
                pl.BlockSpec((1, s, DK_R), lambda bi, h, lg: (bi, 0, k0 + h)),
                pl.BlockSpec((1, s, DV_R), lambda bi, h, lg: (bi, 0, v0 + h)),
                pl.BlockSpec((1, s, DV_R), lambda bi, h, lg: (bi, 0, gf0 + h)),
                pl.BlockSpec((1, s, DV_R), lambda bi, h, lg: (bi, 0, gb0 + h)),
            ],
            out_specs=pl.BlockSpec((1, s, DV_R), lambda bi, h, lg: (bi, 0, h)),
            scratch_shapes=[pltpu.VMEM((2, DK_R, DV_R), F32), pltpu.VMEM((2, s, DV_R), F32)],
        ),
        compiler_params=_params("parallel", "parallel"),
        name="retention",
    )(log_gamma, proj, proj, proj, proj, proj)


def _proj_out_kernel(x_ref, y_ref, w_ref, g_ref, out_ref):
    out_ref[...] = x_ref[...] + _rms(_dot(y_ref[...], w_ref[...]), g_ref[...])


def _proj_out(x2d, y2d, w_out, g):
    t = x2d.shape[0]
    k = y2d.shape[1]
    row = lambda i: (i, 0)
    fixed = lambda i: (0, 0)
    return pl.pallas_call(
        _proj_out_kernel,
        out_shape=jax.ShapeDtypeStruct((t, D_MODEL), F32),
        grid=(t // OUT_ROWS,),
        in_specs=[
            pl.BlockSpec((OUT_ROWS, D_MODEL), row),
            pl.BlockSpec((OUT_ROWS, k), row),
            _resident((k, D_MODEL), fixed),
            _resident((1, D_MODEL), fixed),
        ],
        out_specs=pl.BlockSpec((OUT_ROWS, D_MODEL), row),
        compiler_params=_params("parallel"),
        name="proj_out",
    )(x2d, y2d, w_out, g)


def _rope_tables(s):
    half = DK_R // 2
    inv_freq = 1.0 / (ROPE_BASE ** jnp.linspace(0.0, 1.0, half, dtype=F32))
    ang = jnp.arange(s, dtype=F32)[:, None] * inv_freq[None, :]
    return jnp.cos(ang), jnp.sin(ang)


def kernel(x, norm_gains, ffn_w_gate, ffn_w_up, ffn_w_down, attn_w_in, attn_w_out,
           rel_bias, ret_w_in, ret_w_out, ret_decay_logit):
    b, s, d = x.shape
    t = b * s
    gains = norm_gains.astype(F32).reshape(DEPTH, -1, 1, D_MODEL)
    w_gate = ffn_w_gate.astype(BF16)
    w_up = ffn_w_up.astype(BF16)
    w_down = ffn_w_down.astype(BF16)
    a_w_in = attn_w_in.astype(BF16)
    a_w_out = attn_w_out.astype(BF16)
    r_w_in = ret_w_in.astype(BF16)
    r_w_out = ret_w_out.astype(BF16)
    bias_tiles = [_bias_tiles(rel_bias, g, dil, s // dil)
                  for g, (_, dil) in enumerate(DILATED_GROUPS)]
    cos, sin = _rope_tables(s)
    log_gamma = jnp.log1p(-jnp.exp(ret_decay_logit.astype(F32)))

    x2d = x.reshape(t, d)
    for i in range(DEPTH):
        g = gains[i]
        x2d = _ffn(x2d, g[0], w_gate[i, 0], w_up[i, 0], w_down[i, 0], g[1])
        j = i // 2
        if i % 2 == 0:
            x3d = x2d.reshape(b, s, d)
            qkvs = _attn_proj(x3d, g[2], a_w_in[j])
            outs, maxima, sums = zip(*[_attn_group(qkv, bt) for qkv, bt in zip(qkvs, bias_tiles)])
            x2d = _attn_out(x3d, outs, maxima, sums, a_w_out[j], g[3]).reshape(t, d)
        else:
            proj = _ret_proj(x2d, g[2], r_w_in[j], cos, sin, s).reshape(b, s, RET_IN_COLS)
            y = _retention(proj, log_gamma[j])
            x2d = _proj_out(x2d, y.reshape(t, D_V), r_w_out[j], g[3])
        x2d = _ffn(x2d, g[4], w_gate[i, 1], w_up[i, 1], w_down[i, 1], g[5])
    return x2d.reshape(b, s, d)
```

```python
import functools

import jax
import jax.numpy as jnp
import numpy as np
from jax import lax
from jax.experimental import pallas as pl
from jax.experimental.pallas import tpu as pltpu

D_MODEL = 1024
DEPTH = 4
NORM_EPS = 1e-6
D_FF = 2816
DILATED_GROUPS = ((128, 1), (512, 4), (2048, 16))
N_GROUPS = len(DILATED_GROUPS)
H_A = 16
DH_A = 64
D_A = H_A * DH_A
GROUP_COLS = 3 * D_A
ATTN_IN_COLS = N_GROUPS * GROUP_COLS
NEG_INF = -1e30
NUM_BUCKETS = 32
REL_MAX_DISTANCE = 1024
H_R = 4
DK_R = D_MODEL // H_R
DV_R = 2 * D_MODEL // H_R
D_V = H_R * DV_R
RET_QK_COLS = H_R * DK_R
RET_IN_COLS = 2 * RET_QK_COLS + 3 * D_V
ROPE_BASE = 10000.0

V7X_LANES = 128
V7X_VMEM_LIMIT_BYTES = 56 * 1024 * 1024

FFN_ROWS = 512
FFN_COLS = 256
PROJ_ROWS = 512
PROJ_COLS = 1024
ATTN_Q = 128
ATTN_RADIUS = 64
ATTN_UNITS = 4
RET_SCAN_CHUNK = 256
OUT_ROWS = 512
RET_UNROLL = 2
LANE_SLABS = D_MODEL // V7X_LANES
HEAD_PAIRS = D_A // V7X_LANES

BF16 = jnp.bfloat16
F32 = jnp.float32


def _params(*semantics):
    return pltpu.CompilerParams(dimension_semantics=semantics,
                                vmem_limit_bytes=V7X_VMEM_LIMIT_BYTES)


def _resident(shape, index_map):
    return pl.BlockSpec(shape, index_map, pipeline_mode=pl.Buffered(1))


def _member(shape, lead):
    index = tuple(lead) + (0,) * len(shape)
    return _resident((None,) * len(lead) + tuple(shape), lambda *_: index)


def _rms(x, g):
    ms = jnp.mean(x * x, axis=-1, keepdims=True)
    return x * lax.rsqrt(ms + NORM_EPS) * g


def _dot(a, b):
    return jnp.dot(a, b, preferred_element_type=F32)


def _dot_nt(a, b):
    return lax.dot_general(a, b, (((1,), (1,)), ((), ())), preferred_element_type=F32)


def _dot_tn(a, b):
    return lax.dot_general(a, b, (((0,), (0,)), ((), ())), preferred_element_type=F32)


def _ffn_kernel(x_ref, gpre_ref, wg_ref, wu_ref, wd_ref, gpost_ref, o_ref, acc_ref):
    x = x_ref[...]
    h = _rms(x, gpre_ref[...]).astype(BF16)
    for c in range(D_FF // FFN_COLS):
        cols = slice(c * FFN_COLS, (c + 1) * FFN_COLS)
        gate = _dot(h, wg_ref[:, cols])
        up = _dot(h, wu_ref[:, cols])
        act = (gate * jax.nn.sigmoid(gate) * up).astype(BF16)
        part = _dot(act, wd_ref[cols, :])
        if c == 0:
            acc_ref[...] = part
        else:
            acc_ref[...] += part
    o_ref[...] = x + 0.5 * _rms(acc_ref[...], gpost_ref[...])


def _ffn(x2d, gains, w_gate, w_up, w_down, layer, which):
    t = x2d.shape[0]
    row = lambda i: (i, 0)
    pre, post = (0, 1) if which == 0 else (4, 5)
    return pl.pallas_call(
        _ffn_kernel,
        out_shape=jax.ShapeDtypeStruct((t, D_MODEL), F32),
        grid=(t // FFN_ROWS,),
        in_specs=[
            pl.BlockSpec((FFN_ROWS, D_MODEL), row),
            _member((1, D_MODEL), (layer, pre)),
            _member((D_MODEL, D_FF), (layer, which)),
            _member((D_MODEL, D_FF), (layer, which)),
            _member((D_FF, D_MODEL), (layer, which)),
            _member((1, D_MODEL), (layer, post)),
        ],
        out_specs=pl.BlockSpec((FFN_ROWS, D_MODEL), row),
        scratch_shapes=[pltpu.VMEM((FFN_ROWS, D_MODEL), F32)],
        compiler_params=_params("parallel"),
        name="ffn",
    )(x2d, gains, w_gate, w_up, w_down, gains)


def _qkv_chunk(h, w_ref, grp, c):
    first = grp * GROUP_COLS + c * PROJ_COLS
    res = _dot(h, w_ref[:, first:first + PROJ_COLS])
    if c * PROJ_COLS < D_A:
        res = res * (DH_A ** -0.5)
    return res.astype(BF16)


def _attn_proj_kernel(x_ref, g_ref, w_ref, o0_ref, o1_ref, o2_ref, slab_ref):
    rows = x_ref.shape[1]
    h = _rms(x_ref[0], g_ref[...])
    for j in range(LANE_SLABS):
        slab_ref[j] = h[:, j * V7X_LANES:(j + 1) * V7X_LANES]
    hb = h.astype(BF16)
    for c in range(GROUP_COLS // PROJ_COLS):
        cols = slice(c * PROJ_COLS, (c + 1) * PROJ_COLS)
        o0_ref[0, 0, :, cols] = _qkv_chunk(hb, w_ref, 0, c)
    for grp, o_ref in ((1, o1_ref), (2, o2_ref)):
        dil = DILATED_GROUPS[grp][1]
        n = rows // dil
        by_residue = [
            jnp.concatenate([slab_ref[j, pl.ds(r, n, stride=dil), :] for j in range(LANE_SLABS)], axis=-1)
            for r in range(dil)]
        hp = jnp.concatenate(by_residue, axis=0).astype(BF16)
        for c in range(GROUP_COLS // PROJ_COLS):
            cols = slice(c * PROJ_COLS, (c + 1) * PROJ_COLS)
            res = _qkv_chunk(hp, w_ref, grp, c)
            for r in range(dil):
                o_ref[0, r, :, cols] = res[r * n:(r + 1) * n, :]


def _attn_proj(x, gains, w_in, layer):
    b, s, _ = x.shape
    shapes, specs = [], []
    for _, dil in DILATED_GROUPS:
        shapes.append(jax.ShapeDtypeStruct((b, dil, s // dil, GROUP_COLS), BF16))
        specs.append(pl.BlockSpec((1, dil, PROJ_ROWS // dil, GROUP_COLS), lambda bi, i: (bi, 0, i, 0)))
    return pl.pallas_call(
        _attn_proj_kernel,
        out_shape=tuple(shapes),
        grid=(b, s // PROJ_ROWS),
        in_specs=[
            pl.BlockSpec((1, PROJ_ROWS, D_MODEL), lambda bi, i: (bi, i, 0)),
            _member((1, D_MODEL), (layer, 2)),
            _member((D_MODEL, ATTN_IN_COLS), (layer // 2,)),
        ],
        out_specs=tuple(specs),
        scratch_shapes=[pltpu.VMEM((LANE_SLABS, PROJ_ROWS, V7X_LANES), F32)],
        compiler_params=_params("parallel", "parallel"),
        name="attn_proj",
    )(x, gains, w_in)


def _t5_buckets(rel):
    half = NUM_BUCKETS // 2
    max_exact = half // 2
    n = np.abs(rel)
    large = max_exact + (np.log(np.maximum(n, 1) / max_exact)
                         / np.log(REL_MAX_DISTANCE / max_exact) * (half - max_exact)).astype(np.int64)
    large = np.minimum(large, half - 1)
    return ((rel > 0) * half + np.where(n < max_exact, n, large)).astype(np.int32)


def _attn_window(ls):
    return min(ls, ATTN_Q + 2 * ATTN_RADIUS)


def _window_shifts(ls):
    w = _attn_window(ls)
    if w == ls:
        return (0,)
    return (0, -ATTN_RADIUS, ATTN_Q - w)


def _bias_tiles(rel_bias, g, dilation, ls):
    w = _attn_window(ls)
    n = w + ATTN_Q
    buckets = _t5_buckets(np.arange(-ATTN_RADIUS, ATTN_RADIUS + 1) * dilation)
    bias_g = rel_bias[g * H_A:(g + 1) * H_A][:, buckets].astype(F32)
    m = np.arange(n)
    key_minus_query = np.where(m < w, m, m - n)
    tiles = []
    for shift in _window_shifts(ls):
        off = shift + key_minus_query
        valid = np.abs(off) <= ATTN_RADIUS
        idx = np.clip(off + ATTN_RADIUS, 0, 2 * ATTN_RADIUS)
        band = jnp.where(valid[None], bias_g[:, idx], NEG_INF)
        flat = jnp.tile(band, (1, ATTN_Q))[:, :ATTN_Q * (n - 1)]
        tiles.append(flat.reshape(H_A, ATTN_Q, n - 1)[:, :, :w])
    return jnp.stack(tiles, axis=0)


def _stat_lane(pair, sub):
    return DH_A + pair if sub == 0 else pair


def _attn_kernel(q_ref, k_ref, v_ref, bias_ref, o_ref, m_ref, l_ref, *, ls, residues, blocks):
    w = _attn_window(ls)
    n_blocks = ls // ATTN_Q
    i = pl.program_id(2)
    subs = V7X_LANES // DH_A
    heads = [(pair, sub) for pair in range(HEAD_PAIRS) for sub in range(subs)]
    q_lane = lax.broadcasted_iota(jnp.int32, (ATTN_Q, V7X_LANES), 1)
    v_lane = lax.broadcasted_iota(jnp.int32, (w, V7X_LANES), 1)
    q_half = [q_lane < DH_A, q_lane >= DH_A]
    v_half = [v_lane < DH_A, v_lane >= DH_A]

    def window(u):
        if w == ls:
            return 0, 0
        blk = i * blocks + u
        start = pl.multiple_of(jnp.clip(blk * ATTN_Q - ATTN_RADIUS, 0, ls - w), ATTN_RADIUS)
        return start, jnp.where(blk == 0, 0, jnp.where(blk == n_blocks - 1, 2, 1))

    def scores_of(r, u):
        start, variant = window(u)
        q_rows = slice(u * ATTN_Q, (u + 1) * ATTN_Q)
        scores = {}
        for pair, sub in heads:
            cols = slice(pair * V7X_LANES, (pair + 1) * V7X_LANES)
            qp = q_ref[0, r, q_rows, cols]
            qh = jnp.where(q_half[sub], qp, jnp.zeros_like(qp))
            kp = k_ref[0, r, pl.ds(start, w), cols]
            scores[pair, sub] = _dot_nt(qh, kp) + bias_ref[variant, pair * subs + sub]
        return scores

    def finish(r, u, scores):
        start, _ = window(u)
        q_rows = slice(u * ATTN_Q, (u + 1) * ATTN_Q)
        maxima, probs = {}, {}
        for head, s in scores.items():
            maxima[head] = jnp.max(s, axis=-1, keepdims=True)
            probs[head] = jnp.exp(s - maxima[head]).astype(BF16)
        m_tile = jnp.zeros((ATTN_Q, V7X_LANES), F32)
        l_tile = jnp.ones((ATTN_Q, V7X_LANES), F32)
        for pair in range(HEAD_PAIRS):
            cols = slice(pair * V7X_LANES, (pair + 1) * V7X_LANES)
            vp = v_ref[0, r, pl.ds(start, w), cols]
            out = None
            for sub in range(subs):
                vh = jnp.where(v_half[sub], vp, jnp.ones_like(vp))
                pv = _dot(probs[pair, sub], vh)
                here = q_lane == _stat_lane(pair, sub)
                l_tile = jnp.where(here, pv, l_tile)
                m_tile = jnp.where(here, maxima[pair, sub], m_tile)
                out = pv if out is None else jnp.where(q_half[sub], pv, out)
            o_ref[0, r, q_rows, cols] = out.astype(BF16)
        m_ref[0, r, q_rows, :] = m_tile
        l_ref[0, r, q_rows, :] = l_tile

    units = [(r, u) for r in range(residues) for u in range(blocks)]
    pending = scores_of(*units[0])
    for idx, unit in enumerate(units):
        following = scores_of(*units[idx + 1]) if idx + 1 < len(units) else None
        finish(*unit, pending)
        pending = following


def _attn_group(qkv, bias_tiles):
    b, dil, ls, _ = qkv.shape
    w = _attn_window(ls)
    nvar = bias_tiles.shape[0]
    blocks = min(ATTN_UNITS, ls // ATTN_Q)
    residues = ATTN_UNITS // blocks
    rows = blocks * ATTN_Q
    blk = lambda bi, r, i: (bi, r, i, 0)
    return pl.pallas_call(
        functools.partial(_attn_kernel, ls=ls, residues=residues, blocks=blocks),
        out_shape=(jax.ShapeDtypeStruct((b, dil, ls, D_A), BF16),
                   jax.ShapeDtypeStruct((b, dil, ls, V7X_LANES), F32),
                   jax.ShapeDtypeStruct((b, dil, ls, V7X_LANES), F32)),
        grid=(b, dil // residues, ls // rows),
        in_specs=[
            pl.BlockSpec((1, residues, rows, D_A), blk),
            pl.BlockSpec((1, residues, ls, D_A), lambda bi, r, i: (bi, r, 0, 1)),
            pl.BlockSpec((1, residues, ls, D_A), lambda bi, r, i: (bi, r, 0, 2)),
            _resident((nvar, H_A, ATTN_Q, w), lambda bi, r, i: (0, 0, 0, 0)),
        ],
        out_specs=(pl.BlockSpec((1, residues, rows, D_A), blk),
                   pl.BlockSpec((1, residues, rows, V7X_LANES), blk),
                   pl.BlockSpec((1, residues, rows, V7X_LANES), blk)),
        compiler_params=_params("parallel", "parallel", "arbitrary"),
        name=f"attn_group_d{dil}",
    )(qkv, qkv, qkv, bias_tiles)


def _attn_out_kernel(x_ref, o0_ref, o1_ref, o2_ref, m0_ref, m1_ref, m2_ref, l0_ref, l1_ref, l2_ref,
                     expand_ref, w_ref, g_ref, out_ref, onat_ref, snat_ref):
    rows = x_ref.shape[1]
    for slot, (o_ref, m_ref, l_ref) in enumerate(((o1_ref, m1_ref, l1_ref), (o2_ref, m2_ref, l2_ref))):
        dil = o_ref.shape[1]
        n = rows // dil
        for r in range(dil):
            token_rows = pl.ds(r, n, stride=dil)
            snat_ref[slot, 0, token_rows, :] = m_ref[0, r]
            snat_ref[slot, 1, token_rows, :] = l_ref[0, r]
            blk = o_ref[0, r].astype(F32)
            for j in range(LANE_SLABS):
                onat_ref[slot, j, token_rows, :] = blk[:, j * V7X_LANES:(j + 1) * V7X_LANES]
    maxima = [m0_ref[0, 0], snat_ref[0, 0], snat_ref[1, 0]]
    sums = [l0_ref[0, 0], snat_ref[0, 1], snat_ref[1, 1]]
    outs = [o0_ref[0, 0].astype(F32)] + [
        jnp.concatenate([onat_ref[slot, j] for j in range(LANE_SLABS)], axis=-1) for slot in range(2)]
    top = jnp.maximum(jnp.maximum(maxima[0], maxima[1]), maxima[2])
    scale = [jnp.exp(m - top) for m in maxima]
    denom = scale[0] * sums[0] + scale[1] * sums[1] + scale[2] * sums[2]
    merged = None
    for sc, o in zip(scale, outs):
        wt = (sc / denom).astype(BF16)
        wt_wide = _dot(wt, expand_ref[...])
        term = wt_wide * o
        merged = term if merged is None else merged + term
    y = _dot(merged.astype(BF16), w_ref[...])
    out_ref[0] = x_ref[0] + _rms(y, g_ref[...])


def _attn_out(x, outs, maxima, sums, w_out, gains, layer):
    b, s, _ = x.shape
    head_of_col = np.arange(D_A) // DH_A
    lane_of_col = np.array([_stat_lane(h // 2, h % 2) for h in head_of_col])
    expand = jnp.asarray(np.arange(V7X_LANES)[:, None] == lane_of_col[None, :], BF16)
    tok = lambda bi, i: (bi, i, 0)
    grp = lambda bi, i: (bi, 0, i, 0)
    fixed = lambda bi, i: (0, 0)
    dils = [dil for _, dil in DILATED_GROUPS]
    stat_specs = [pl.BlockSpec((1, dil, OUT_ROWS // dil, V7X_LANES), grp) for dil in dils]
    return pl.pallas_call(
        _attn_out_kernel,
        out_shape=jax.ShapeDtypeStruct((b, s, D_MODEL), F32),
        grid=(b, s // OUT_ROWS),
        in_specs=[pl.BlockSpec((1, OUT_ROWS, D_MODEL), tok)]
        + [pl.BlockSpec((1, dil, OUT_ROWS // dil, D_A), grp) for dil in dils]
        + stat_specs + stat_specs
        + [_resident((V7X_LANES, D_A), fixed),
           _member((D_A, D_MODEL), (layer // 2,)),
           _member((1, D_MODEL), (layer, 3))],
        out_specs=pl.BlockSpec((1, OUT_ROWS, D_MODEL), tok),
        scratch_shapes=[pltpu.VMEM((2, LANE_SLABS, OUT_ROWS, V7X_LANES), F32),
                        pltpu.VMEM((2, 2, OUT_ROWS, V7X_LANES), F32)],
        compiler_params=_params("parallel", "parallel"),
        name="attn_out",
    )(x, *outs, *maxima, *sums, expand, w_out, gains)


def _rotary(t, cos, sin):
    half = DK_R // 2
    parts = []
    for h in range(t.shape[1] // DK_R):
        t1 = t[:, h * DK_R:h * DK_R + half]
        t2 = t[:, h * DK_R + half:(h + 1) * DK_R]
        parts += [t1 * cos - t2 * sin, t1 * sin + t2 * cos]
    return jnp.concatenate(parts, axis=-1)


def _ret_proj_kernel(x_ref, g_ref, w_ref, cos_ref, sin_ref, o_ref):
    h = _rms(x_ref[...], g_ref[...]).astype(BF16)
    for c in range(RET_IN_COLS // PROJ_COLS):
        first = c * PROJ_COLS
        cols = slice(first, first + PROJ_COLS)
        res = _dot(h, w_ref[:, cols])
        if first < 2 * RET_QK_COLS:
            res = _rotary(res, cos_ref[...], sin_ref[...])
            if first >= RET_QK_COLS:
                res = res * (DK_R ** -0.5)
        o_ref[:, cols] = res.astype(BF16)


def _ret_proj(x2d, gains, w_in, cos, sin, s, layer):
    t = x2d.shape[0]
    assert RET_QK_COLS % PROJ_COLS == 0 and PROJ_COLS % DK_R == 0
    row = lambda i: (i, 0)
    pos = lambda i: (i % (s // PROJ_ROWS), 0)
    return pl.pallas_call(
        _ret_proj_kernel,
        out_shape=jax.ShapeDtypeStruct((t, RET_IN_COLS), BF16),
        grid=(t // PROJ_ROWS,),
        in_specs=[
            pl.BlockSpec((PROJ_ROWS, D_MODEL), row),
            _member((1, D_MODEL), (layer, 2)),
            _member((D_MODEL, RET_IN_COLS), (layer // 2,)),
            pl.BlockSpec((PROJ_ROWS, DK_R // 2), pos),
            pl.BlockSpec((PROJ_ROWS, DK_R // 2), pos),
        ],
        out_specs=pl.BlockSpec((PROJ_ROWS, RET_IN_COLS), row),
        compiler_params=_params("parallel"),
        name="ret_proj",
    )(x2d, gains, w_in, cos, sin)


def _group_norm(y):
    mu = jnp.mean(y, axis=-1, keepdims=True)
    d = y - mu
    var = jnp.mean(d * d, axis=-1, keepdims=True)
    return d * lax.rsqrt(var + NORM_EPS)


def _retention_kernel(lg_ref, q_ref, k_ref, v_ref, gf_ref, gb_ref, o_ref, state_ref, y_ref):
    c_len = RET_SCAN_CHUNK
    n_chunks = q_ref.shape[1] // c_len
    head = pl.program_id(1)
    ii = lax.broadcasted_iota(jnp.int32, (c_len, c_len), 0).astype(F32)
    jj = lax.broadcasted_iota(jnp.int32, (c_len, c_len), 1).astype(F32)
    col = lax.broadcasted_iota(jnp.int32, (c_len, 1), 0).astype(F32)
    dirs = (0, 1)
    gate_refs = (gf_ref, gb_ref)
    dmat, q_decay, k_decay, chunk_decay = [], [], [], []
    for d in dirs:
        lg = lg_ref[d, head]
        rel = ii - jj if d == 0 else jj - ii
        dmat.append(jnp.where(rel >= 0, jnp.exp(lg * jnp.maximum(rel, 0.0)), 0.0))
        q_decay.append(jnp.exp(lg * (col + 1.0 if d == 0 else c_len - col)))
        k_decay.append(jnp.exp(lg * (c_len - 1.0 - col if d == 0 else col)))
        chunk_decay.append(jnp.exp(lg * c_len))
    state_ref[...] = jnp.zeros_like(state_ref)

    def step(n, finish):
        rows = [pl.ds(pl.multiple_of(chunk * c_len, c_len), c_len) for chunk in (n, n_chunks - 1 - n)]
        q = [q_ref[0, rows[d], :] for d in dirs]
        k = [k_ref[0, rows[d], :] for d in dirs]
        v = [v_ref[0, rows[d], :] for d in dirs]
        state = [state_ref[d] for d in dirs]
        scores = [_dot_nt(q[d], k[d]) for d in dirs]
        cross = [_dot(q[d], state[d].astype(BF16)) for d in dirs]
        probs = [(scores[d] * dmat[d]).astype(BF16) for d in dirs]
        kd = [(k[d].astype(F32) * k_decay[d]).astype(BF16) for d in dirs]
        inner = [_dot(probs[d], v[d]) for d in dirs]
        update = [_dot_tn(kd[d], v[d]) for d in dirs]
        for d in dirs:
            state_ref[d] = state[d] * chunk_decay[d] + update[d]
        for d in dirs:
            gate = gate_refs[d][0, rows[d], :].astype(F32)
            y = gate * jax.nn.sigmoid(gate) * _group_norm(inner[d] + cross[d] * q_decay[d])
            if finish:
                o_ref[0, rows[d], :] = (y + y_ref[1 - d, rows[d], :]).astype(BF16)
            else:
                y_ref[d, rows[d], :] = y

    def first_half(n, carry):
        step(n, False)
        return carry

    def second_half(n, carry):
        step(n, True)
        return carry

    lax.fori_loop(0, n_chunks // 2, first_half, 0, unroll=RET_UNROLL)
    lax.fori_loop(n_chunks // 2, n_chunks, second_half, 0, unroll=RET_UNROLL)


def _retention(proj, log_gamma):
    b, s, _ = proj.shape
    assert s % (2 * RET_UNROLL * RET_SCAN_CHUNK) == 0
    q0 = 0
    k0 = H_R
    v0 = (2 * RET_QK_COLS) // DV_R
    gf0 = v0 + H_R
    gb0 = gf0 + H_R
    return pl.pallas_call(
        _retention_kernel,
        out_shape=jax.ShapeDtypeStruct((b, s, D_V), BF16),
        grid_spec=pltpu.PrefetchScalarGridSpec(
            num_scalar_prefetch=1,
            grid=(b, H_R),
            in_specs=[
                pl.BlockSpec((1, s, DK_R), lambda bi, h, lg: (bi, 0, q0 + h)),
                pl.BlockSpec((1, s, DK_R), lambda bi, h, lg: (bi, 0, k0 + h)),
                pl.BlockSpec((1, s, DV_R), lambda bi, h, lg: (bi, 0, v0 + h)),
                pl.BlockSpec((1, s, DV_R), lambda bi, h, lg: (bi, 0, gf0 + h)),
                pl.BlockSpec((1, s, DV_R), lambda bi, h, lg: (bi, 0, gb0 + h)),
            ],
            out_specs=pl.BlockSpec((1, s, DV_R), lambda bi, h, lg: (bi, 0, h)),
            scratch_shapes=[pltpu.VMEM((2, DK_R, DV_R), F32), pltpu.VMEM((2, s, DV_R), F32)],
        ),
        compiler_params=_params("parallel", "parallel"),
        name="retention",
    )(log_gamma, proj, proj, proj, proj, proj)


def _proj_out_kernel(x_ref, y_ref, w_ref, g_ref, out_ref):
    out_ref[...] = x_ref[...] + _rms(_dot(y_ref[...], w_ref[...]), g_ref[...])


def _proj_out(x2d, y2d, w_out, gains, layer):
    t = x2d.shape[0]
    k = y2d.shape[1]
    row = lambda i: (i, 0)
    return pl.pallas_call(
        _proj_out_kernel,
        out_shape=jax.ShapeDtypeStruct((t, D_MODEL), F32),
        grid=(t // OUT_ROWS,),
        in_specs=[
            pl.BlockSpec((OUT_ROWS, D_MODEL), row),
            pl.BlockSpec((OUT_ROWS, k), row),
            _member((k, D_MODEL), (layer // 2,)),
            _member((1, D_MODEL), (layer, 3)),
        ],
        out_specs=pl.BlockSpec((OUT_ROWS, D_MODEL), row),
        compiler_params=_params("parallel"),
        name="proj_out",
    )(x2d, y2d, w_out, gains)


def _rope_tables(s):
    half = DK_R // 2
    inv_freq = 1.0 / (ROPE_BASE ** jnp.linspace(0.0, 1.0, half, dtype=F32))
    ang = jnp.arange(s, dtype=F32)[:, None] * inv_freq[None, :]
    return jnp.cos(ang), jnp.sin(ang)


def kernel(x, norm_gains, ffn_w_gate, ffn_w_up, ffn_w_down, attn_w_in, attn_w_out,
           rel_bias, ret_w_in, ret_w_out, ret_decay_logit):
    b, s, d = x.shape
    t = b * s
    gains = norm_gains.astype(F32).reshape(DEPTH, -1, 1, D_MODEL)
    w_gate = ffn_w_gate.astype(BF16)
    w_up = ffn_w_up.astype(BF16)
    w_down = ffn_w_down.astype(BF16)
    a_w_in = attn_w_in.astype(BF16)
    a_w_out = attn_w_out.astype(BF16)
    r_w_in = ret_w_in.astype(BF16)
    r_w_out = ret_w_out.astype(BF16)
    bias_tiles = [_bias_tiles(rel_bias, g, dil, s // dil)
                  for g, (_, dil) in enumerate(DILATED_GROUPS)]
    cos, sin = _rope_tables(s)
    log_gamma = jnp.log1p(-jnp.exp(ret_decay_logit.astype(F32)))

    x2d = x.reshape(t, d)
    for i in range(DEPTH):
        x2d = _ffn(x2d, gains, w_gate, w_up, w_down, i, 0)
        if i % 2 == 0:
            x3d = x2d.reshape(b, s, d)
            qkvs = _attn_proj(x3d, gains, a_w_in, i)
            outs, maxima, sums = zip(*[_attn_group(qkv, bt) for qkv, bt in zip(qkvs, bias_tiles)])
            x2d = _attn_out(x3d, outs, maxima, sums, a_w_out, gains, i).reshape(t, d)
        else:
            proj = _ret_proj(x2d, gains, r_w_in, cos, sin, s, i).reshape(b, s, RET_IN_COLS)
            y = _retention(proj, log_gamma[i // 2])
            x2d = _proj_out(x2d, y.reshape(t, D_V), r_w_out, gains, i)
        x2d = _ffn(x2d, gains, w_gate, w_up, w_down, i, 1)
    return x2d.reshape(b, s, d)
```

```python
import functools

import jax
import jax.numpy as jnp
import numpy as np
from jax import lax
from jax.experimental import pallas as pl
from jax.experimental.pallas import tpu as pltpu

D_MODEL = 1024
DEPTH = 4
NORM_EPS = 1e-6
D_FF = 2816
DILATED_GROUPS = ((128, 1), (512, 4), (2048, 16))
N_GROUPS = len(DILATED_GROUPS)
H_A = 16
DH_A = 64
D_A = H_A * DH_A
GROUP_COLS = 3 * D_A
ATTN_IN_COLS = N_GROUPS * GROUP_COLS
NEG_INF = -1e30
NUM_BUCKETS = 32
REL_MAX_DISTANCE = 1024
H_R = 4
DK_R = D_MODEL // H_R
DV_R = 2 * D_MODEL // H_R
D_V = H_R * DV_R
RET_QK_COLS = H_R * DK_R
RET_IN_COLS = 2 * RET_QK_COLS + 3 * D_V
ROPE_BASE = 10000.0

V7X_LANES = 128
V7X_VMEM_LIMIT_BYTES = 56 * 1024 * 1024

FFN_ROWS = 1024
FFN_COLS = 256
PROJ_ROWS = 512
PROJ_COLS = 1024
ATTN_Q = 128
ATTN_RADIUS = 64
ATTN_UNITS = 8
RET_SCAN_CHUNK = 256
OUT_ROWS = 512
RET_OUT_ROWS = 1024
RET_UNROLL = 2
LANE_SLABS = D_MODEL // V7X_LANES
HEAD_PAIRS = D_A // V7X_LANES

BF16 = jnp.bfloat16
F32 = jnp.float32


def _params(*semantics):
    return pltpu.CompilerParams(dimension_semantics=semantics,
                                vmem_limit_bytes=V7X_VMEM_LIMIT_BYTES)


def _resident(shape, index_map):
    return pl.BlockSpec(shape, index_map, pipeline_mode=pl.Buffered(1))


def _member(shape, lead):
    index = tuple(lead) + (0,) * len(shape)
    return _resident((None,) * len(lead) + tuple(shape), lambda *_: index)


def _rms(x, g):
    ms = jnp.mean(x * x, axis=-1, keepdims=True)
    return x * lax.rsqrt(ms + NORM_EPS) * g


def _dot(a, b):
    return jnp.dot(a, b, preferred_element_type=F32)


def _dot_nt(a, b):
    return lax.dot_general(a, b, (((1,), (1,)), ((), ())), preferred_element_type=F32)


def _dot_tn(a, b):
    return lax.dot_general(a, b, (((0,), (0,)), ((), ())), preferred_element_type=F32)


def _ffn_kernel(x_ref, gpre_ref, wg_ref, wu_ref, wd_ref, gpost_ref, o_ref, acc_ref):
    x = x_ref[...]
    h = _rms(x, gpre_ref[...]).astype(BF16)
    for c in range(D_FF // FFN_COLS):
        cols = slice(c * FFN_COLS, (c + 1) * FFN_COLS)
        gate = _dot(h, wg_ref[:, cols])
        up = _dot(h, wu_ref[:, cols])
        act = (gate * jax.nn.sigmoid(gate) * up).astype(BF16)
        part = _dot(act, wd_ref[cols, :])
        if c == 0:
            acc_ref[...] = part
        else:
            acc_ref[...] += part
    o_ref[...] = x + 0.5 * _rms(acc_ref[...], gpost_ref[...])


def _ffn(x2d, gains, w_gate, w_up, w_down, layer, which):
    t = x2d.shape[0]
    row = lambda i: (i, 0)
    pre, post = (0, 1) if which == 0 else (4, 5)
    return pl.pallas_call(
        _ffn_kernel,
        out_shape=jax.ShapeDtypeStruct((t, D_MODEL), F32),
        grid=(t // FFN_ROWS,),
        in_specs=[
            pl.BlockSpec((FFN_ROWS, D_MODEL), row),
            _member((1, D_MODEL), (layer, pre)),
            _member((D_MODEL, D_FF), (layer, which)),
            _member((D_MODEL, D_FF), (layer, which)),
            _member((D_FF, D_MODEL), (layer, which)),
            _member((1, D_MODEL), (layer, post)),
        ],
        out_specs=pl.BlockSpec((FFN_ROWS, D_MODEL), row),
        scratch_shapes=[pltpu.VMEM((FFN_ROWS, D_MODEL), F32)],
        compiler_params=_params("parallel"),
        name="ffn",
    )(x2d, gains, w_gate, w_up, w_down, gains)


def _qkv_chunk(h, w_ref, grp, c):
    first = grp * GROUP_COLS + c * PROJ_COLS
    res = _dot(h, w_ref[:, first:first + PROJ_COLS])
    if c * PROJ_COLS < D_A:
        res = res * (DH_A ** -0.5)
    return res.astype(BF16)


def _attn_proj_kernel(x_ref, g_ref, w_ref, o0_ref, o1_ref, o2_ref, slab_ref):
    rows = x_ref.shape[1]
    h = _rms(x_ref[0], g_ref[...])
    for j in range(LANE_SLABS):
        slab_ref[j] = h[:, j * V7X_LANES:(j + 1) * V7X_LANES]
    hb = h.astype(BF16)
    for c in range(GROUP_COLS // PROJ_COLS):
        cols = slice(c * PROJ_COLS, (c + 1) * PROJ_COLS)
        o0_ref[0, 0, :, cols] = _qkv_chunk(hb, w_ref, 0, c)
    for grp, o_ref in ((1, o1_ref), (2, o2_ref)):
        dil = DILATED_GROUPS[grp][1]
        n = rows // dil
        by_residue = [
            jnp.concatenate([slab_ref[j, pl.ds(r, n, stride=dil), :] for j in range(LANE_SLABS)], axis=-1)
            for r in range(dil)]
        hp = jnp.concatenate(by_residue, axis=0).astype(BF16)
        for c in range(GROUP_COLS // PROJ_COLS):
            cols = slice(c * PROJ_COLS, (c + 1) * PROJ_COLS)
            res = _qkv_chunk(hp, w_ref, grp, c)
            for r in range(dil):
                o_ref[0, r, :, cols] = res[r * n:(r + 1) * n, :]


def _attn_proj(x, gains, w_in, layer):
    b, s, _ = x.shape
    shapes, specs = [], []
    for _, dil in DILATED_GROUPS:
        shapes.append(jax.ShapeDtypeStruct((b, dil, s // dil, GROUP_COLS), BF16))
        specs.append(pl.BlockSpec((1, dil, PROJ_ROWS // dil, GROUP_COLS), lambda bi, i: (bi, 0, i, 0)))
    return pl.pallas_call(
        _attn_proj_kernel,
        out_shape=tuple(shapes),
        grid=(b, s // PROJ_ROWS),
        in_specs=[
            pl.BlockSpec((1, PROJ_ROWS, D_MODEL), lambda bi, i: (bi, i, 0)),
            _member((1, D_MODEL), (layer, 2)),
            _member((D_MODEL, ATTN_IN_COLS), (layer // 2,)),
        ],
        out_specs=tuple(specs),
        scratch_shapes=[pltpu.VMEM((LANE_SLABS, PROJ_ROWS, V7X_LANES), F32)],
        compiler_params=_params("parallel", "parallel"),
        name="attn_proj",
    )(x, gains, w_in)


def _t5_buckets(rel):
    half = NUM_BUCKETS // 2
    max_exact = half // 2
    n = np.abs(rel)
    large = max_exact + (np.log(np.maximum(n, 1) / max_exact)
                         / np.log(REL_MAX_DISTANCE / max_exact) * (half - max_exact)).astype(np.int64)
    large = np.minimum(large, half - 1)
    return ((rel > 0) * half + np.where(n < max_exact, n, large)).astype(np.int32)


def _attn_window(ls):
    return min(ls, ATTN_Q + 2 * ATTN_RADIUS)


def _window_shifts(ls):
    w = _attn_window(ls)
    if w == ls:
        return (0,)
    return (0, -ATTN_RADIUS, ATTN_Q - w)


def _bias_tiles(rel_bias, g, dilation, ls):
    w = _attn_window(ls)
    n = w + ATTN_Q
    buckets = _t5_buckets(np.arange(-ATTN_RADIUS, ATTN_RADIUS + 1) * dilation)
    bias_g = rel_bias[g * H_A:(g + 1) * H_A][:, buckets].astype(F32)
    m = np.arange(n)
    key_minus_query = np.where(m < w, m, m - n)
    tiles = []
    for shift in _window_shifts(ls):
        off = shift + key_minus_query
        valid = np.abs(off) <= ATTN_RADIUS
        idx = np.clip(off + ATTN_RADIUS, 0, 2 * ATTN_RADIUS)
        band = jnp.where(valid[None], bias_g[:, idx], NEG_INF)
        flat = jnp.tile(band, (1, ATTN_Q))[:, :ATTN_Q * (n - 1)]
        tiles.append(flat.reshape(H_A, ATTN_Q, n - 1)[:, :, :w])
    return jnp.stack(tiles, axis=0)


def _stat_lane(pair, sub):
    return DH_A + pair if sub == 0 else pair


def _attn_kernel(q_ref, k_ref, v_ref, bias_ref, o_ref, m_ref, l_ref, *, ls, residues, blocks):
    w = _attn_window(ls)
    n_blocks = ls // ATTN_Q
    i = pl.program_id(2)
    subs = V7X_LANES // DH_A
    heads = [(pair, sub) for pair in range(HEAD_PAIRS) for sub in range(subs)]
    q_lane = lax.broadcasted_iota(jnp.int32, (ATTN_Q, V7X_LANES), 1)
    v_lane = lax.broadcasted_iota(jnp.int32, (w, V7X_LANES), 1)
    q_half = [q_lane < DH_A, q_lane >= DH_A]
    v_half = [v_lane < DH_A, v_lane >= DH_A]

    def window(u):
        if w == ls:
            return 0, 0
        blk = i * blocks + u
        start = pl.multiple_of(jnp.clip(blk * ATTN_Q - ATTN_RADIUS, 0, ls - w), ATTN_RADIUS)
        return start, jnp.where(blk == 0, 0, jnp.where(blk == n_blocks - 1, 2, 1))

    def scores_of(r, u):
        start, variant = window(u)
        q_rows = slice(u * ATTN_Q, (u + 1) * ATTN_Q)
        scores = {}
        for pair, sub in heads:
            cols = slice(pair * V7X_LANES, (pair + 1) * V7X_LANES)
            qp = q_ref[0, r, q_rows, cols]
            qh = jnp.where(q_half[sub], qp, jnp.zeros_like(qp))
            kp = k_ref[0, r, pl.ds(start, w), cols]
            scores[pair, sub] = _dot_nt(qh, kp) + bias_ref[variant, pair * subs + sub]
        return scores

    def finish(r, u, scores):
        start, _ = window(u)
        q_rows = slice(u * ATTN_Q, (u + 1) * ATTN_Q)
        maxima, probs = {}, {}
        for head, s in scores.items():
            maxima[head] = jnp.max(s, axis=-1, keepdims=True)
            probs[head] = jnp.exp(s - maxima[head]).astype(BF16)
        m_tile = jnp.zeros((ATTN_Q, V7X_LANES), F32)
        l_tile = jnp.ones((ATTN_Q, V7X_LANES), F32)
        for pair in range(HEAD_PAIRS):
            cols = slice(pair * V7X_LANES, (pair + 1) * V7X_LANES)
            vp = v_ref[0, r, pl.ds(start, w), cols]
            out = None
            for sub in range(subs):
                vh = jnp.where(v_half[sub], vp, jnp.ones_like(vp))
                pv = _dot(probs[pair, sub], vh)
                here = q_lane == _stat_lane(pair, sub)
                l_tile = jnp.where(here, pv, l_tile)
                m_tile = jnp.where(here, maxima[pair, sub], m_tile)
                out = pv if out is None else jnp.where(q_half[sub], pv, out)
            o_ref[0, r, q_rows, cols] = out.astype(BF16)
        m_ref[0, r, q_rows, :] = m_tile
        l_ref[0, r, q_rows, :] = l_tile

    units = [(r, u) for r in range(residues) for u in range(blocks)]
    pending = scores_of(*units[0])
    for idx, unit in enumerate(units):
        following = scores_of(*units[idx + 1]) if idx + 1 < len(units) else None
        finish(*unit, pending)
        pending = following


def _attn_group(qkv, bias_tiles):
    b, dil, ls, _ = qkv.shape
    w = _attn_window(ls)
    nvar = bias_tiles.shape[0]
    blocks = min(ATTN_UNITS, ls // ATTN_Q)
    residues = ATTN_UNITS // blocks
    rows = blocks * ATTN_Q
    blk = lambda bi, r, i: (bi, r, i, 0)
    return pl.pallas_call(
        functools.partial(_attn_kernel, ls=ls, residues=residues, blocks=blocks),
        out_shape=(jax.ShapeDtypeStruct((b, dil, ls, D_A), BF16),
                   jax.ShapeDtypeStruct((b, dil, ls, V7X_LANES), F32),
                   jax.ShapeDtypeStruct((b, dil, ls, V7X_LANES), F32)),
        grid=(b, dil // residues, ls // rows),
        in_specs=[
            pl.BlockSpec((1, residues, rows, D_A), blk),
            pl.BlockSpec((1, residues, ls, D_A), lambda bi, r, i: (bi, r, 0, 1)),
            pl.BlockSpec((1, residues, ls, D_A), lambda bi, r, i: (bi, r, 0, 2)),
            _resident((nvar, H_A, ATTN_Q, w), lambda bi, r, i: (0, 0, 0, 0)),
        ],
        out_specs=(pl.BlockSpec((1, residues, rows, D_A), blk),
                   pl.BlockSpec((1, residues, rows, V7X_LANES), blk),
                   pl.BlockSpec((1, residues, rows, V7X_LANES), blk)),
        compiler_params=_params("parallel", "parallel", "arbitrary"),
        name=f"attn_group_d{dil}",
    )(qkv, qkv, qkv, bias_tiles)


def _attn_out_kernel(x_ref, o0_ref, o1_ref, o2_ref, m0_ref, m1_ref, m2_ref, l0_ref, l1_ref, l2_ref,
                     expand_ref, w_ref, g_ref, out_ref, onat_ref, snat_ref):
    rows = x_ref.shape[1]
    for slot, (o_ref, m_ref, l_ref) in enumerate(((o1_ref, m1_ref, l1_ref), (o2_ref, m2_ref, l2_ref))):
        dil = o_ref.shape[1]
        n = rows // dil
        for r in range(dil):
            token_rows = pl.ds(r, n, stride=dil)
            snat_ref[slot, 0, token_rows, :] = m_ref[0, r]
            snat_ref[slot, 1, token_rows, :] = l_ref[0, r]
            blk = o_ref[0, r].astype(F32)
            for j in range(LANE_SLABS):
                onat_ref[slot, j, token_rows, :] = blk[:, j * V7X_LANES:(j + 1) * V7X_LANES]
    maxima = [m0_ref[0, 0], snat_ref[0, 0], snat_ref[1, 0]]
    sums = [l0_ref[0, 0], snat_ref[0, 1], snat_ref[1, 1]]
    outs = [o0_ref[0, 0].astype(F32)] + [
        jnp.concatenate([onat_ref[slot, j] for j in range(LANE_SLABS)], axis=-1) for slot in range(2)]
    top = jnp.maximum(jnp.maximum(maxima[0], maxima[1]), maxima[2])
    scale = [jnp.exp(m - top) for m in maxima]
    denom = scale[0] * sums[0] + scale[1] * sums[1] + scale[2] * sums[2]
    merged = None
    for sc, o in zip(scale, outs):
        wt = (sc / denom).astype(BF16)
        wt_wide = _dot(wt, expand_ref[...])
        term = wt_wide * o
        merged = term if merged is None else merged + term
    y = _dot(merged.astype(BF16), w_ref[...])
    out_ref[0] = x_ref[0] + _rms(y, g_ref[...])


def _attn_out(x, outs, maxima, sums, w_out, gains, layer):
    b, s, _ = x.shape
    head_of_col = np.arange(D_A) // DH_A
    lane_of_col = np.array([_stat_lane(h // 2, h % 2) for h in head_of_col])
    expand = jnp.asarray(np.arange(V7X_LANES)[:, None] == lane_of_col[None, :], BF16)
    tok = lambda bi, i: (bi, i, 0)
    grp = lambda bi, i: (bi, 0, i, 0)
    fixed = lambda bi, i: (0, 0)
    dils = [dil for _, dil in DILATED_GROUPS]
    stat_specs = [pl.BlockSpec((1, dil, OUT_ROWS // dil, V7X_LANES), grp) for dil in dils]
    return pl.pallas_call(
        _attn_out_kernel,
        out_shape=jax.ShapeDtypeStruct((b, s, D_MODEL), F32),
        grid=(b, s // OUT_ROWS),
        in_specs=[pl.BlockSpec((1, OUT_ROWS, D_MODEL), tok)]
        + [pl.BlockSpec((1, dil, OUT_ROWS // dil, D_A), grp) for dil in dils]
        + stat_specs + stat_specs
        + [_resident((V7X_LANES, D_A), fixed),
           _member((D_A, D_MODEL), (layer // 2,)),
           _member((1, D_MODEL), (layer, 3))],
        out_specs=pl.BlockSpec((1, OUT_ROWS, D_MODEL), tok),
        scratch_shapes=[pltpu.VMEM((2, LANE_SLABS, OUT_ROWS, V7X_LANES), F32),
                        pltpu.VMEM((2, 2, OUT_ROWS, V7X_LANES), F32)],
        compiler_params=_params("parallel", "parallel"),
        name="attn_out",
    )(x, *outs, *maxima, *sums, expand, w_out, gains)


def _rotary(t, cos, sin):
    half = DK_R // 2
    parts = []
    for h in range(t.shape[1] // DK_R):
        t1 = t[:, h * DK_R:h * DK_R + half]
        t2 = t[:, h * DK_R + half:(h + 1) * DK_R]
        parts += [t1 * cos - t2 * sin, t1 * sin + t2 * cos]
    return jnp.concatenate(parts, axis=-1)


def _ret_proj_kernel(x_ref, g_ref, w_ref, cos_ref, sin_ref, o_ref):
    h = _rms(x_ref[...], g_ref[...]).astype(BF16)
    for c in range(RET_IN_COLS // PROJ_COLS):
        first = c * PROJ_COLS
        cols = slice(first, first + PROJ_COLS)
        res = _dot(h, w_ref[:, cols])
        if first < 2 * RET_QK_COLS:
            res = _rotary(res, cos_ref[...], sin_ref[...])
            if first >= RET_QK_COLS:
                res = res * (DK_R ** -0.5)
        o_ref[:, cols] = res.astype(BF16)


def _ret_proj(x2d, gains, w_in, cos, sin, s, layer):
    t = x2d.shape[0]
    assert RET_QK_COLS % PROJ_COLS == 0 and PROJ_COLS % DK_R == 0
    row = lambda i: (i, 0)
    pos = lambda i: (i % (s // PROJ_ROWS), 0)
    return pl.pallas_call(
        _ret_proj_kernel,
        out_shape=jax.ShapeDtypeStruct((t, RET_IN_COLS), BF16),
        grid=(t // PROJ_ROWS,),
        in_specs=[
            pl.BlockSpec((PROJ_ROWS, D_MODEL), row),
            _member((1, D_MODEL), (layer, 2)),
            _member((D_MODEL, RET_IN_COLS), (layer // 2,)),
            pl.BlockSpec((PROJ_ROWS, DK_R // 2), pos),
            pl.BlockSpec((PROJ_ROWS, DK_R // 2), pos),
        ],
        out_specs=pl.BlockSpec((PROJ_ROWS, RET_IN_COLS), row),
        compiler_params=_params("parallel"),
        name="ret_proj",
    )(x2d, gains, w_in, cos, sin)


def _group_norm(y):
    mu = jnp.mean(y, axis=-1, keepdims=True)
    d = y - mu
    var = jnp.mean(d * d, axis=-1, keepdims=True)
    return d * lax.rsqrt(var + NORM_EPS)


def _retention_kernel(lg_ref, q_ref, k_ref, v_ref, gf_ref, gb_ref, o_ref, state_ref, y_ref):
    c_len = RET_SCAN_CHUNK
    n_chunks = q_ref.shape[1] // c_len
    head = pl.program_id(1)
    ii = lax.broadcasted_iota(jnp.int32, (c_len, c_len), 0).astype(F32)
    jj = lax.broadcasted_iota(jnp.int32, (c_len, c_len), 1).astype(F32)
    col = lax.broadcasted_iota(jnp.int32, (c_len, 1), 0).astype(F32)
    dirs = (0, 1)
    gate_refs = (gf_ref, gb_ref)
    dmat, q_decay, k_decay, chunk_decay = [], [], [], []
    for d in dirs:
        lg = lg_ref[d, head]
        rel = ii - jj if d == 0 else jj - ii
        dmat.append(jnp.where(rel >= 0, jnp.exp(lg * jnp.maximum(rel, 0.0)), 0.0))
        q_decay.append(jnp.exp(lg * (col + 1.0 if d == 0 else c_len - col)))
        k_decay.append(jnp.exp(lg * (c_len - 1.0 - col if d == 0 else col)))
        chunk_decay.append(jnp.exp(lg * c_len))
    state_ref[...] = jnp.zeros_like(state_ref)

    def step(n, finish):
        rows = [pl.ds(pl.multiple_of(chunk * c_len, c_len), c_len) for chunk in (n, n_chunks - 1 - n)]
        q = [q_ref[0, rows[d], :] for d in dirs]
        k = [k_ref[0, rows[d], :] for d in dirs]
        v = [v_ref[0, rows[d], :] for d in dirs]
        state = [state_ref[d] for d in dirs]
        scores = [_dot_nt(q[d], k[d]) for d in dirs]
        cross = [_dot(q[d], state[d].astype(BF16)) for d in dirs]
        probs = [(scores[d] * dmat[d]).astype(BF16) for d in dirs]
        kd = [(k[d].astype(F32) * k_decay[d]).astype(BF16) for d in dirs]
        inner = [_dot(probs[d], v[d]) for d in dirs]
        update = [_dot_tn(kd[d], v[d]) for d in dirs]
        for d in dirs:
            state_ref[d] = state[d] * chunk_decay[d] + update[d]
        for d in dirs:
            gate = gate_refs[d][0, rows[d], :].astype(F32)
            y = gate * jax.nn.sigmoid(gate) * _group_norm(inner[d] + cross[d] * q_decay[d])
            if finish:
                o_ref[0, rows[d], :] = (y + y_ref[1 - d, rows[d], :]).astype(BF16)
            else:
                y_ref[d, rows[d], :] = y

    def first_half(n, carry):
        step(n, False)
        return carry

    def second_half(n, carry):
        step(n, True)
        return carry

    lax.fori_loop(0, n_chunks // 2, first_half, 0, unroll=RET_UNROLL)
    lax.fori_loop(n_chunks // 2, n_chunks, second_half, 0, unroll=RET_UNROLL)


def _retention(proj, log_gamma):
    b, s, _ = proj.shape
    assert s % (2 * RET_UNROLL * RET_SCAN_CHUNK) == 0
    q0 = 0
    k0 = H_R
    v0 = (2 * RET_QK_COLS) // DV_R
    gf0 = v0 + H_R
    gb0 = gf0 + H_R
    return pl.pallas_call(
        _retention_kernel,
        out_shape=jax.ShapeDtypeStruct((b, s, D_V), BF16),
        grid_spec=pltpu.PrefetchScalarGridSpec(
            num_scalar_prefetch=1,
            grid=(b, H_R),
            in_specs=[
                pl.BlockSpec((1, s, DK_R), lambda bi, h, lg: (bi, 0, q0 + h)),
                pl.BlockSpec((1, s, DK_R), lambda bi, h, lg: (bi, 0, k0 + h)),
                pl.BlockSpec((1, s, DV_R), lambda bi, h, lg: (bi, 0, v0 + h)),
                pl.BlockSpec((1, s, DV_R), lambda bi, h, lg: (bi, 0, gf0 + h)),
                pl.BlockSpec((1, s, DV_R), lambda bi, h, lg: (bi, 0, gb0 + h)),
            ],
            out_specs=pl.BlockSpec((1, s, DV_R), lambda bi, h, lg: (bi, 0, h)),
            scratch_shapes=[pltpu.VMEM((2, DK_R, DV_R), F32), pltpu.VMEM((2, s, DV_R), F32)],
        ),
        compiler_params=_params("parallel", "parallel"),
        name="retention",
    )(log_gamma, proj, proj, proj, proj, proj)


def _proj_out_kernel(x_ref, y_ref, w_ref, g_ref, out_ref):
    out_ref[...] = x_ref[...] + _rms(_dot(y_ref[...], w_ref[...]), g_ref[...])


def _proj_out(x2d, y2d, w_out, gains, layer):
    t = x2d.shape[0]
    k = y2d.shape[1]
    row = lambda i: (i, 0)
    return pl.pallas_call(
        _proj_out_kernel,
        out_shape=jax.ShapeDtypeStruct((t, D_MODEL), F32),
        grid=(t // RET_OUT_ROWS,),
        in_specs=[
            pl.BlockSpec((RET_OUT_ROWS, D_MODEL), row),
            pl.BlockSpec((RET_OUT_ROWS, k), row),
            _member((k, D_MODEL), (layer // 2,)),
            _member((1, D_MODEL), (layer, 3)),
        ],
        out_specs=pl.BlockSpec((RET_OUT_ROWS, D_MODEL), row),
        compiler_params=_params("parallel"),
        name="proj_out",
    )(x2d, y2d, w_out, gains)


def _rope_tables(s):
    half = DK_R // 2
    inv_freq = 1.0 / (ROPE_BASE ** jnp.linspace(0.0, 1.0, half, dtype=F32))
    ang = jnp.arange(s, dtype=F32)[:, None] * inv_freq[None, :]
    return jnp.cos(ang), jnp.sin(ang)


def kernel(x, norm_gains, ffn_w_gate, ffn_w_up, ffn_w_down, attn_w_in, attn_w_out,
           rel_bias, ret_w_in, ret_w_out, ret_decay_logit):
    b, s, d = x.shape
    t = b * s
    gains = norm_gains.astype(F32).reshape(DEPTH, -1, 1, D_MODEL)
    w_gate = ffn_w_gate.astype(BF16)
    w_up = ffn_w_up.astype(BF16)
    w_down = ffn_w_down.astype(BF16)
    a_w_in = attn_w_in.astype(BF16)
    a_w_out = attn_w_out.astype(BF16)
    r_w_in = ret_w_in.astype(BF16)
    r_w_out = ret_w_out.astype(BF16)
    bias_tiles = [_bias_tiles(rel_bias, g, dil, s // dil)
                  for g, (_, dil) in enumerate(DILATED_GROUPS)]
    cos, sin = _rope_tables(s)
    log_gamma = jnp.log1p(-jnp.exp(ret_decay_logit.astype(F32)))

    x2d = x.reshape(t, d)
    for i in range(DEPTH):
        x2d = _ffn(x2d, gains, w_gate, w_up, w_down, i, 0)
        if i % 2 == 0:
            x3d = x2d.reshape(b, s, d)
            qkvs = _attn_proj(x3d, gains, a_w_in, i)
            outs, maxima, sums = zip(*[_attn_group(qkv, bt) for qkv, bt in zip(qkvs, bias_tiles)])
            x2d = _attn_out(x3d, outs, maxima, sums, a_w_out, gains, i).reshape(t, d)
        else:
            proj = _ret_proj(x2d, gains, r_w_in, cos, sin, s, i).reshape(b, s, RET_IN_COLS)
            y = _retention(proj, log_gamma[i // 2])
            x2d = _proj_out(x2d, y.reshape(t, D_V), r_w_out, gains, i)
        x2d = _ffn(x2d, gains, w_gate, w_up, w_down, i, 1)
    return x2d.reshape(b, s, d)
```

```python
import functools

import jax
import jax.numpy as jnp
import numpy as np
from jax import lax
from jax.experimental import pallas as pl
from jax.experimental.pallas import tpu as pltpu

D_MODEL = 1024
DEPTH = 4
NORM_EPS = 1e-6
FFN_NORMS = ((0, 1), (4, 5))
MIXER_PRE_NORM, MIXER_POST_NORM = 2, 3
D_FF = 2816
DILATED_GROUPS = ((128, 1), (512, 4), (2048, 16))
N_GROUPS = len(DILATED_GROUPS)
H_A = 16
DH_A = 64
D_A = H_A * DH_A
GROUP_COLS = 3 * D_A
ATTN_IN_COLS = N_GROUPS * GROUP_COLS
NEG_INF = -1e30
NUM_BUCKETS = 32
REL_MAX_DISTANCE = 1024
H_R = 4
DK_R = D_MODEL // H_R
DV_R = 2 * D_MODEL // H_R
D_V = H_R * DV_R
RET_QK_COLS = H_R * DK_R
RET_IN_COLS = 2 * RET_QK_COLS + 3 * D_V
ROPE_BASE = 10000.0

V7X_LANES = 128
V7X_VMEM_LIMIT_BYTES = 56 * 1024 * 1024

FFN_ROWS = 1024
FFN_COLS = 256
PROJ_ROWS = 512
PROJ_COLS = 1024
ATTN_Q = 128
ATTN_RADIUS = 64
ATTN_UNITS = 8
RET_SCAN_CHUNK = 256
OUT_ROWS = 512
RET_OUT_ROWS = 1024
RET_UNROLL = 4
LANE_SLABS = D_MODEL // V7X_LANES
HEAD_PAIRS = D_A // V7X_LANES

BF16 = jnp.bfloat16
F32 = jnp.float32


def _params(*semantics):
    return pltpu.CompilerParams(dimension_semantics=semantics,
                                vmem_limit_bytes=V7X_VMEM_LIMIT_BYTES)


def _resident(shape, index_map):
    return pl.BlockSpec(shape, index_map, pipeline_mode=pl.Buffered(1))


def _member(shape, lead):
    index = tuple(lead) + (0,) * len(shape)
    return _resident((None,) * len(lead) + tuple(shape), lambda *_: index)


def _rms(x, g):
    ms = jnp.mean(x * x, axis=-1, keepdims=True)
    return x * lax.rsqrt(ms + NORM_EPS) * g


def _dot(a, b):
    return jnp.dot(a, b, preferred_element_type=F32)


def _dot_nt(a, b):
    return lax.dot_general(a, b, (((1,), (1,)), ((), ())), preferred_element_type=F32)


def _dot_tn(a, b):
    return lax.dot_general(a, b, (((0,), (0,)), ((), ())), preferred_element_type=F32)


def _ffn_kernel(x_ref, gpre_ref, wg_ref, wu_ref, wd_ref, gpost_ref, o_ref, acc_ref):
    x = x_ref[...]
    h = _rms(x, gpre_ref[...]).astype(BF16)
    for c in range(D_FF // FFN_COLS):
        cols = slice(c * FFN_COLS, (c + 1) * FFN_COLS)
        gate = _dot(h, wg_ref[:, cols])
        up = _dot(h, wu_ref[:, cols])
        act = (gate * jax.nn.sigmoid(gate) * up).astype(BF16)
        part = _dot(act, wd_ref[cols, :])
        if c == 0:
            acc_ref[...] = part
        else:
            acc_ref[...] += part
    o_ref[...] = x + 0.5 * _rms(acc_ref[...], gpost_ref[...])


def _ffn(x2d, gains, w_gate, w_up, w_down, layer, which):
    t = x2d.shape[0]
    row = lambda i: (i, 0)
    pre, post = FFN_NORMS[which]
    return pl.pallas_call(
        _ffn_kernel,
        out_shape=jax.ShapeDtypeStruct((t, D_MODEL), F32),
        grid=(t // FFN_ROWS,),
        in_specs=[
            pl.BlockSpec((FFN_ROWS, D_MODEL), row),
            _member((1, D_MODEL), (layer, pre)),
            _member((D_MODEL, D_FF), (layer, which)),
            _member((D_MODEL, D_FF), (layer, which)),
            _member((D_FF, D_MODEL), (layer, which)),
            _member((1, D_MODEL), (layer, post)),
        ],
        out_specs=pl.BlockSpec((FFN_ROWS, D_MODEL), row),
        scratch_shapes=[pltpu.VMEM((FFN_ROWS, D_MODEL), F32)],
        compiler_params=_params("parallel"),
        name="ffn",
    )(x2d, gains, w_gate, w_up, w_down, gains)


def _qkv_chunk(h, w_ref, grp, c):
    first = grp * GROUP_COLS + c * PROJ_COLS
    res = _dot(h, w_ref[:, first:first + PROJ_COLS])
    if c * PROJ_COLS < D_A:
        res = res * (DH_A ** -0.5)
    return res.astype(BF16)


def _attn_proj_kernel(x_ref, g_ref, w_ref, o0_ref, o1_ref, o2_ref, slab_ref):
    rows = x_ref.shape[1]
    h = _rms(x_ref[0], g_ref[...])
    for j in range(LANE_SLABS):
        slab_ref[j] = h[:, j * V7X_LANES:(j + 1) * V7X_LANES]
    hb = h.astype(BF16)
    for c in range(GROUP_COLS // PROJ_COLS):
        cols = slice(c * PROJ_COLS, (c + 1) * PROJ_COLS)
        o0_ref[0, 0, :, cols] = _qkv_chunk(hb, w_ref, 0, c)
    for grp, o_ref in ((1, o1_ref), (2, o2_ref)):
        dil = DILATED_GROUPS[grp][1]
        n = rows // dil
        by_residue = [
            jnp.concatenate([slab_ref[j, pl.ds(r, n, stride=dil), :] for j in range(LANE_SLABS)], axis=-1)
            for r in range(dil)]
        hp = jnp.concatenate(by_residue, axis=0).astype(BF16)
        for c in range(GROUP_COLS // PROJ_COLS):
            cols = slice(c * PROJ_COLS, (c + 1) * PROJ_COLS)
            res = _qkv_chunk(hp, w_ref, grp, c)
            for r in range(dil):
                o_ref[0, r, :, cols] = res[r * n:(r + 1) * n, :]


def _attn_proj(x, gains, w_in, layer):
    b, s, _ = x.shape
    shapes, specs = [], []
    for _, dil in DILATED_GROUPS:
        shapes.append(jax.ShapeDtypeStruct((b, dil, s // dil, GROUP_COLS), BF16))
        specs.append(pl.BlockSpec((1, dil, PROJ_ROWS // dil, GROUP_COLS), lambda bi, i: (bi, 0, i, 0)))
    return pl.pallas_call(
        _attn_proj_kernel,
        out_shape=tuple(shapes),
        grid=(b, s // PROJ_ROWS),
        in_specs=[
            pl.BlockSpec((1, PROJ_ROWS, D_MODEL), lambda bi, i: (bi, i, 0)),
            _member((1, D_MODEL), (layer, MIXER_PRE_NORM)),
            _member((D_MODEL, ATTN_IN_COLS), (layer // 2,)),
        ],
        out_specs=tuple(specs),
        scratch_shapes=[pltpu.VMEM((LANE_SLABS, PROJ_ROWS, V7X_LANES), F32)],
        compiler_params=_params("parallel", "parallel"),
        name="attn_proj",
    )(x, gains, w_in)


def _t5_buckets(rel):
    half = NUM_BUCKETS // 2
    max_exact = half // 2
    n = np.abs(rel)
    large = max_exact + (np.log(np.maximum(n, 1) / max_exact)
                         / np.log(REL_MAX_DISTANCE / max_exact) * (half - max_exact)).astype(np.int64)
    large = np.minimum(large, half - 1)
    return ((rel > 0) * half + np.where(n < max_exact, n, large)).astype(np.int32)


def _attn_window(ls):
    return min(ls, ATTN_Q + 2 * ATTN_RADIUS)


def _window_shifts(ls):
    w = _attn_window(ls)
    if w == ls:
        return (0,)
    return (0, -ATTN_RADIUS, ATTN_Q - w)


def _bias_tiles(rel_bias, g, dilation, ls):
    w = _attn_window(ls)
    n = w + ATTN_Q
    buckets = _t5_buckets(np.arange(-ATTN_RADIUS, ATTN_RADIUS + 1) * dilation)
    bias_g = rel_bias[g * H_A:(g + 1) * H_A][:, buckets].astype(F32)
    m = np.arange(n)
    key_minus_query = np.where(m < w, m, m - n)
    tiles = []
    for shift in _window_shifts(ls):
        off = shift + key_minus_query
        valid = np.abs(off) <= ATTN_RADIUS
        idx = np.clip(off + ATTN_RADIUS, 0, 2 * ATTN_RADIUS)
        band = jnp.where(valid[None], bias_g[:, idx], NEG_INF)
        flat = jnp.tile(band, (1, ATTN_Q))[:, :ATTN_Q * (n - 1)]
        tiles.append(flat.reshape(H_A, ATTN_Q, n - 1)[:, :, :w])
    return jnp.stack(tiles, axis=0)


def _stat_lane(pair, sub):
    return DH_A + pair if sub == 0 else pair


def _attn_kernel(q_ref, k_ref, v_ref, bias_ref, o_ref, m_ref, l_ref, *, ls, residues, blocks):
    w = _attn_window(ls)
    n_blocks = ls // ATTN_Q
    i = pl.program_id(2)
    subs = V7X_LANES // DH_A
    heads = [(pair, sub) for pair in range(HEAD_PAIRS) for sub in range(subs)]
    q_lane = lax.broadcasted_iota(jnp.int32, (ATTN_Q, V7X_LANES), 1)
    v_lane = lax.broadcasted_iota(jnp.int32, (w, V7X_LANES), 1)
    q_half = [q_lane < DH_A, q_lane >= DH_A]
    v_half = [v_lane < DH_A, v_lane >= DH_A]

    def window(u):
        if w == ls:
            return 0, 0
        blk = i * blocks + u
        start = pl.multiple_of(jnp.clip(blk * ATTN_Q - ATTN_RADIUS, 0, ls - w), ATTN_RADIUS)
        return start, jnp.where(blk == 0, 0, jnp.where(blk == n_blocks - 1, 2, 1))

    def scores_of(r, u):
        start, variant = window(u)
        q_rows = slice(u * ATTN_Q, (u + 1) * ATTN_Q)
        scores = {}
        for pair, sub in heads:
            cols = slice(pair * V7X_LANES, (pair + 1) * V7X_LANES)
            qp = q_ref[0, r, q_rows, cols]
            qh = jnp.where(q_half[sub], qp, jnp.zeros_like(qp))
            kp = k_ref[0, r, pl.ds(start, w), cols]
            scores[pair, sub] = _dot_nt(qh, kp) + bias_ref[variant, pair * subs + sub]
        return scores

    def finish(r, u, scores):
        start, _ = window(u)
        q_rows = slice(u * ATTN_Q, (u + 1) * ATTN_Q)
        maxima, probs = {}, {}
        for head, s in scores.items():
            maxima[head] = jnp.max(s, axis=-1, keepdims=True)
            probs[head] = jnp.exp(s - maxima[head]).astype(BF16)
        m_tile = jnp.zeros((ATTN_Q, V7X_LANES), F32)
        l_tile = jnp.ones((ATTN_Q, V7X_LANES), F32)
        for pair in range(HEAD_PAIRS):
            cols = slice(pair * V7X_LANES, (pair + 1) * V7X_LANES)
            vp = v_ref[0, r, pl.ds(start, w), cols]
            out = None
            for sub in range(subs):
                vh = jnp.where(v_half[sub], vp, jnp.ones_like(vp))
                pv = _dot(probs[pair, sub], vh)
                here = q_lane == _stat_lane(pair, sub)
                l_tile = jnp.where(here, pv, l_tile)
                m_tile = jnp.where(here, maxima[pair, sub], m_tile)
                out = pv if out is None else jnp.where(q_half[sub], pv, out)
            o_ref[0, r, q_rows, cols] = out.astype(BF16)
        m_ref[0, r, q_rows, :] = m_tile
        l_ref[0, r, q_rows, :] = l_tile

    units = [(r, u) for r in range(residues) for u in range(blocks)]
    pending = scores_of(*units[0])
    for idx, unit in enumerate(units):
        following = scores_of(*units[idx + 1]) if idx + 1 < len(units) else None
        finish(*unit, pending)
        pending = following


def _attn_group(qkv, bias_tiles):
    b, dil, ls, _ = qkv.shape
    w = _attn_window(ls)
    nvar = bias_tiles.shape[0]
    blocks = min(ATTN_UNITS, ls // ATTN_Q)
    residues = ATTN_UNITS // blocks
    rows = blocks * ATTN_Q
    blk = lambda bi, r, i: (bi, r, i, 0)
    return pl.pallas_call(
        functools.partial(_attn_kernel, ls=ls, residues=residues, blocks=blocks),
        out_shape=(jax.ShapeDtypeStruct((b, dil, ls, D_A), BF16),
                   jax.ShapeDtypeStruct((b, dil, ls, V7X_LANES), F32),
                   jax.ShapeDtypeStruct((b, dil, ls, V7X_LANES), F32)),
        grid=(b, dil // residues, ls // rows),
        in_specs=[
            pl.BlockSpec((1, residues, rows, D_A), blk),
            pl.BlockSpec((1, residues, ls, D_A), lambda bi, r, i: (bi, r, 0, 1)),
            pl.BlockSpec((1, residues, ls, D_A), lambda bi, r, i: (bi, r, 0, 2)),
            _resident((nvar, H_A, ATTN_Q, w), lambda bi, r, i: (0, 0, 0, 0)),
        ],
        out_specs=(pl.BlockSpec((1, residues, rows, D_A), blk),
                   pl.BlockSpec((1, residues, rows, V7X_LANES), blk),
                   pl.BlockSpec((1, residues, rows, V7X_LANES), blk)),
        compiler_params=_params("parallel", "parallel", "arbitrary"),
        name=f"attn_group_d{dil}",
    )(qkv, qkv, qkv, bias_tiles)


def _attn_out_kernel(x_ref, o0_ref, o1_ref, o2_ref, m0_ref, m1_ref, m2_ref, l0_ref, l1_ref, l2_ref,
                     expand_ref, w_ref, g_ref, out_ref, onat_ref, snat_ref):
    rows = x_ref.shape[1]
    for slot, (o_ref, m_ref, l_ref) in enumerate(((o1_ref, m1_ref, l1_ref), (o2_ref, m2_ref, l2_ref))):
        dil = o_ref.shape[1]
        n = rows // dil
        for r in range(dil):
            token_rows = pl.ds(r, n, stride=dil)
            snat_ref[slot, 0, token_rows, :] = m_ref[0, r]
            snat_ref[slot, 1, token_rows, :] = l_ref[0, r]
            blk = o_ref[0, r].astype(F32)
            for j in range(LANE_SLABS):
                onat_ref[slot, j, token_rows, :] = blk[:, j * V7X_LANES:(j + 1) * V7X_LANES]
    maxima = [m0_ref[0, 0], snat_ref[0, 0], snat_ref[1, 0]]
    sums = [l0_ref[0, 0], snat_ref[0, 1], snat_ref[1, 1]]
    outs = [o0_ref[0, 0].astype(F32)] + [
        jnp.concatenate([onat_ref[slot, j] for j in range(LANE_SLABS)], axis=-1) for slot in range(2)]
    top = jnp.maximum(jnp.maximum(maxima[0], maxima[1]), maxima[2])
    scale = [jnp.exp(m - top) for m in maxima]
    denom = scale[0] * sums[0] + scale[1] * sums[1] + scale[2] * sums[2]
    merged = None
    for sc, o in zip(scale, outs):
        wt = (sc / denom).astype(BF16)
        wt_wide = _dot(wt, expand_ref[...])
        term = wt_wide * o
        merged = term if merged is None else merged + term
    y = _dot(merged.astype(BF16), w_ref[...])
    out_ref[0] = x_ref[0] + _rms(y, g_ref[...])


def _attn_out(x, outs, maxima, sums, w_out, gains, layer):
    b, s, _ = x.shape
    head_of_col = np.arange(D_A) // DH_A
    lane_of_col = np.array([_stat_lane(h // 2, h % 2) for h in head_of_col])
    expand = jnp.asarray(np.arange(V7X_LANES)[:, None] == lane_of_col[None, :], BF16)
    tok = lambda bi, i: (bi, i, 0)
    grp = lambda bi, i: (bi, 0, i, 0)
    fixed = lambda bi, i: (0, 0)
    dils = [dil for _, dil in DILATED_GROUPS]
    stat_specs = [pl.BlockSpec((1, dil, OUT_ROWS // dil, V7X_LANES), grp) for dil in dils]
    return pl.pallas_call(
        _attn_out_kernel,
        out_shape=jax.ShapeDtypeStruct((b, s, D_MODEL), F32),
        grid=(b, s // OUT_ROWS),
        in_specs=[pl.BlockSpec((1, OUT_ROWS, D_MODEL), tok)]
        + [pl.BlockSpec((1, dil, OUT_ROWS // dil, D_A), grp) for dil in dils]
        + stat_specs + stat_specs
        + [_resident((V7X_LANES, D_A), fixed),
           _member((D_A, D_MODEL), (layer // 2,)),
           _member((1, D_MODEL), (layer, MIXER_POST_NORM))],
        out_specs=pl.BlockSpec((1, OUT_ROWS, D_MODEL), tok),
        scratch_shapes=[pltpu.VMEM((2, LANE_SLABS, OUT_ROWS, V7X_LANES), F32),
                        pltpu.VMEM((2, 2, OUT_ROWS, V7X_LANES), F32)],
        compiler_params=_params("parallel", "parallel"),
        name="attn_out",
    )(x, *outs, *maxima, *sums, expand, w_out, gains)


def _rotary(t, cos, sin):
    half = DK_R // 2
    parts = []
    for h in range(t.shape[1] // DK_R):
        t1 = t[:, h * DK_R:h * DK_R + half]
        t2 = t[:, h * DK_R + half:(h + 1) * DK_R]
        parts += [t1 * cos - t2 * sin, t1 * sin + t2 * cos]
    return jnp.concatenate(parts, axis=-1)


def _ret_proj_kernel(x_ref, g_ref, w_ref, cos_ref, sin_ref, o_ref):
    h = _rms(x_ref[...], g_ref[...]).astype(BF16)
    for c in range(RET_IN_COLS // PROJ_COLS):
        first = c * PROJ_COLS
        cols = slice(first, first + PROJ_COLS)
        res = _dot(h, w_ref[:, cols])
        if first < 2 * RET_QK_COLS:
            res = _rotary(res, cos_ref[...], sin_ref[...])
            if first >= RET_QK_COLS:
                res = res * (DK_R ** -0.5)
        o_ref[:, cols] = res.astype(BF16)


def _ret_proj(x2d, gains, w_in, cos, sin, s, layer):
    t = x2d.shape[0]
    assert RET_QK_COLS % PROJ_COLS == 0 and PROJ_COLS % DK_R == 0
    row = lambda i: (i, 0)
    pos = lambda i: (i % (s // PROJ_ROWS), 0)
    return pl.pallas_call(
        _ret_proj_kernel,
        out_shape=jax.ShapeDtypeStruct((t, RET_IN_COLS), BF16),
        grid=(t // PROJ_ROWS,),
        in_specs=[
            pl.BlockSpec((PROJ_ROWS, D_MODEL), row),
            _member((1, D_MODEL), (layer, MIXER_PRE_NORM)),
            _member((D_MODEL, RET_IN_COLS), (layer // 2,)),
            pl.BlockSpec((PROJ_ROWS, DK_R // 2), pos),
            pl.BlockSpec((PROJ_ROWS, DK_R // 2), pos),
        ],
        out_specs=pl.BlockSpec((PROJ_ROWS, RET_IN_COLS), row),
        compiler_params=_params("parallel"),
        name="ret_proj",
    )(x2d, gains, w_in, cos, sin)


def _group_norm(y):
    mu = jnp.mean(y, axis=-1, keepdims=True)
    d = y - mu
    var = jnp.mean(d * d, axis=-1, keepdims=True)
    return d * lax.rsqrt(var + NORM_EPS)


def _retention_kernel(lg_ref, q_ref, k_ref, v_ref, gf_ref, gb_ref, o_ref, state_ref, y_ref):
    c_len = RET_SCAN_CHUNK
    n_chunks = q_ref.shape[1] // c_len
    head = pl.program_id(1)
    ii = lax.broadcasted_iota(jnp.int32, (c_len, c_len), 0).astype(F32)
    jj = lax.broadcasted_iota(jnp.int32, (c_len, c_len), 1).astype(F32)
    col = lax.broadcasted_iota(jnp.int32, (c_len, 1), 0).astype(F32)
    dirs = (0, 1)
    gate_refs = (gf_ref, gb_ref)
    dmat, q_decay, k_decay, chunk_decay = [], [], [], []
    for d in dirs:
        lg = lg_ref[d, head]
        rel = ii - jj if d == 0 else jj - ii
        dmat.append(jnp.where(rel >= 0, jnp.exp(lg * jnp.maximum(rel, 0.0)), 0.0))
        q_decay.append(jnp.exp(lg * (col + 1.0 if d == 0 else c_len - col)))
        k_decay.append(jnp.exp(lg * (c_len - 1.0 - col if d == 0 else col)))
        chunk_decay.append(jnp.exp(lg * c_len))
    state_ref[...] = jnp.zeros_like(state_ref)

    def step(n, finish):
        rows = [pl.ds(pl.multiple_of(chunk * c_len, c_len), c_len) for chunk in (n, n_chunks - 1 - n)]
        q = [q_ref[0, rows[d], :] for d in dirs]
        k = [k_ref[0, rows[d], :] for d in dirs]
        v = [v_ref[0, rows[d], :] for d in dirs]
        state = [state_ref[d] for d in dirs]
        scores = [_dot_nt(q[d], k[d]) for d in dirs]
        cross = [_dot(q[d], state[d].astype(BF16)) for d in dirs]
        probs = [(scores[d] * dmat[d]).astype(BF16) for d in dirs]
        kd = [(k[d].astype(F32) * k_decay[d]).astype(BF16) for d in dirs]
        inner = [_dot(probs[d], v[d]) for d in dirs]
        update = [_dot_tn(kd[d], v[d]) for d in dirs]
        for d in dirs:
            state_ref[d] = state[d] * chunk_decay[d] + update[d]
        for d in dirs:
            gate = gate_refs[d][0, rows[d], :].astype(F32)
            y = gate * jax.nn.sigmoid(gate) * _group_norm(inner[d] + cross[d] * q_decay[d])
            if finish:
                o_ref[0, rows[d], :] = (y + y_ref[1 - d, rows[d], :]).astype(BF16)
            else:
                y_ref[d, rows[d], :] = y

    def first_half(n, carry):
        step(n, False)
        return carry

    def second_half(n, carry):
        step(n, True)
        return carry

    lax.fori_loop(0, n_chunks // 2, first_half, 0, unroll=RET_UNROLL)
    lax.fori_loop(n_chunks // 2, n_chunks, second_half, 0, unroll=RET_UNROLL)


def _retention(proj, log_gamma):
    b, s, _ = proj.shape
    assert s % (2 * RET_UNROLL * RET_SCAN_CHUNK) == 0
    q0 = 0
    k0 = H_R
    v0 = (2 * RET_QK_COLS) // DV_R
    gf0 = v0 + H_R
    gb0 = gf0 + H_R
    return pl.pallas_call(
        _retention_kernel,
        out_shape=jax.ShapeDtypeStruct((b, s, D_V), BF16),
        grid_spec=pltpu.PrefetchScalarGridSpec(
            num_scalar_prefetch=1,
            grid=(b, H_R),
            in_specs=[
                pl.BlockSpec((1, s, DK_R), lambda bi, h, lg: (bi, 0, q0 + h)),
                pl.BlockSpec((1, s, DK_R), lambda bi, h, lg: (bi, 0, k0 + h)),
                pl.BlockSpec((1, s, DV_R), lambda bi, h, lg: (bi, 0, v0 + h)),
                pl.BlockSpec((1, s, DV_R), lambda bi, h, lg: (bi, 0, gf0 + h)),
                pl.BlockSpec((1, s, DV_R), lambda bi, h, lg: (bi, 0, gb0 + h)),
            ],
            out_specs=pl.BlockSpec((1, s, DV_R), lambda bi, h, lg: (bi, 0, h)),
            scratch_shapes=[pltpu.VMEM((2, DK_R, DV_R), F32), pltpu.VMEM((2, s, DV_R), F32)],
        ),
        compiler_params=_params("parallel", "parallel"),
        name="retention",
    )(log_gamma, proj, proj, proj, proj, proj)


def _proj_out_kernel(x_ref, y_ref, w_ref, g_ref, out_ref):
    out_ref[...] = x_ref[...] + _rms(_dot(y_ref[...], w_ref[...]), g_ref[...])


def _proj_out(x2d, y2d, w_out, gains, layer):
    t = x2d.shape[0]
    k = y2d.shape[1]
    row = lambda i: (i, 0)
    return pl.pallas_call(
        _proj_out_kernel,
        out_shape=jax.ShapeDtypeStruct((t, D_MODEL), F32),
        grid=(t // RET_OUT_ROWS,),
        in_specs=[
            pl.BlockSpec((RET_OUT_ROWS, D_MODEL), row),
            pl.BlockSpec((RET_OUT_ROWS, k), row),
            _member((k, D_MODEL), (layer // 2,)),
            _member((1, D_MODEL), (layer, MIXER_POST_NORM)),
        ],
        out_specs=pl.BlockSpec((RET_OUT_ROWS, D_MODEL), row),
        compiler_params=_params("parallel"),
        name="proj_out",
    )(x2d, y2d, w_out, gains)


def _rope_tables(s):
    half = DK_R // 2
    inv_freq = 1.0 / (ROPE_BASE ** jnp.linspace(0.0, 1.0, half, dtype=F32))
    ang = jnp.arange(s, dtype=F32)[:, None] * inv_freq[None, :]
    return jnp.cos(ang), jnp.sin(ang)


def kernel(x, norm_gains, ffn_w_gate, ffn_w_up, ffn_w_down, attn_w_in, attn_w_out,
           rel_bias, ret_w_in, ret_w_out, ret_decay_logit):
    b, s, d = x.shape
    t = b * s
    gains = norm_gains.astype(F32).reshape(DEPTH, -1, 1, D_MODEL)
    w_gate = ffn_w_gate.astype(BF16)
    w_up = ffn_w_up.astype(BF16)
    w_down = ffn_w_down.astype(BF16)
    a_w_in = attn_w_in.astype(BF16)
    a_w_out = attn_w_out.astype(BF16)
    r_w_in = ret_w_in.astype(BF16)
    r_w_out = ret_w_out.astype(BF16)
    bias_tiles = [_bias_tiles(rel_bias, g, dil, s // dil)
                  for g, (_, dil) in enumerate(DILATED_GROUPS)]
    cos, sin = _rope_tables(s)
    log_gamma = jnp.log1p(-jnp.exp(ret_decay_logit.astype(F32)))

    x2d = x.reshape(t, d)
    for i in range(DEPTH):
        x2d = _ffn(x2d, gains, w_gate, w_up, w_down, i, 0)
        if i % 2 == 0:
            x3d = x2d.reshape(b, s, d)
            qkvs = _attn_proj(x3d, gains, a_w_in, i)
            outs, maxima, sums = zip(*[_attn_group(qkv, bt) for qkv, bt in zip(qkvs, bias_tiles)])
            x2d = _attn_out(x3d, outs, maxima, sums, a_w_out, gains, i).reshape(t, d)
        else:
            proj = _ret_proj(x2d, gains, r_w_in, cos, sin, s, i).reshape(b, s, RET_IN_COLS)
            y = _retention(proj, log_gamma[i // 2])
            x2d = _proj_out(x2d, y.reshape(t, D_V), r_w_out, gains, i)
        x2d = _ffn(x2d, gains, w_gate, w_up, w_down, i, 1)
    return x2d.reshape(b, s, d)
```

```python
import functools

import jax
import jax.numpy as jnp
import numpy as np
from jax import lax
from jax.experimental import pallas as pl
from jax.experimental.pallas import tpu as pltpu

D_MODEL = 1024
DEPTH = 4
NORM_EPS = 1e-6
FFN_NORMS = ((0, 1), (4, 5))
MIXER_PRE_NORM, MIXER_POST_NORM = 2, 3
D_FF = 2816
DILATED_GROUPS = ((128, 1), (512, 4), (2048, 16))
N_GROUPS = len(DILATED_GROUPS)
H_A = 16
DH_A = 64
D_A = H_A * DH_A
GROUP_COLS = 3 * D_A
ATTN_IN_COLS = N_GROUPS * GROUP_COLS
NEG_INF = -1e30
NUM_BUCKETS = 32
REL_MAX_DISTANCE = 1024
H_R = 4
DK_R = D_MODEL // H_R
DV_R = 2 * D_MODEL // H_R
D_V = H_R * DV_R
RET_QK_COLS = H_R * DK_R
RET_IN_COLS = 2 * RET_QK_COLS + 3 * D_V
ROPE_BASE = 10000.0

V7X_LANES = 128
V7X_VMEM_LIMIT_BYTES = 56 * 1024 * 1024

FFN_ROWS = 1024
FFN_COLS = 256
PROJ_ROWS = 512
PROJ_COLS = 1024
ATTN_Q = 128
ATTN_RADIUS = 64
ATTN_UNITS = 8
RET_SCAN_CHUNK = 256
OUT_ROWS = 512
RET_OUT_ROWS = 1024
RET_UNROLL = 4
LANE_SLABS = D_MODEL // V7X_LANES
HEAD_PAIRS = D_A // V7X_LANES

BF16 = jnp.bfloat16
F32 = jnp.float32


def _params(*semantics):
    return pltpu.CompilerParams(dimension_semantics=semantics,
                                vmem_limit_bytes=V7X_VMEM_LIMIT_BYTES)


def _resident(shape, index_map):
    return pl.BlockSpec(shape, index_map, pipeline_mode=pl.Buffered(1))


def _member(shape, lead):
    index = tuple(lead) + (0,) * len(shape)
    return _resident((None,) * len(lead) + tuple(shape), lambda *_: index)


def _rms(x, g):
    ms = jnp.mean(x * x, axis=-1, keepdims=True)
    return x * lax.rsqrt(ms + NORM_EPS) * g


def _dot(a, b):
    return jnp.dot(a, b, preferred_element_type=F32)


def _dot_nt(a, b):
    return lax.dot_general(a, b, (((1,), (1,)), ((), ())), preferred_element_type=F32)


def _dot_tn(a, b):
    return lax.dot_general(a, b, (((0,), (0,)), ((), ())), preferred_element_type=F32)


def _ffn_kernel(x_ref, gpre_ref, wg_ref, wu_ref, wd_ref, gpost_ref, o_ref, acc_ref):
    x = x_ref[...]
    h = _rms(x, gpre_ref[...]).astype(BF16)
    for c in range(D_FF // FFN_COLS):
        cols = slice(c * FFN_COLS, (c + 1) * FFN_COLS)
        gate = _dot(h, wg_ref[:, cols])
        up = _dot(h, wu_ref[:, cols])
        act = (gate * jax.nn.sigmoid(gate) * up).astype(BF16)
        part = _dot(act, wd_ref[cols, :])
        if c == 0:
            acc_ref[...] = part
        else:
            acc_ref[...] += part
    o_ref[...] = x + 0.5 * _rms(acc_ref[...], gpost_ref[...])


def _ffn(x2d, gains, w_gate, w_up, w_down, layer, which):
    t = x2d.shape[0]
    row = lambda i: (i, 0)
    pre, post = FFN_NORMS[which]
    return pl.pallas_call(
        _ffn_kernel,
        out_shape=jax.ShapeDtypeStruct((t, D_MODEL), F32),
        grid=(t // FFN_ROWS,),
        in_specs=[
            pl.BlockSpec((FFN_ROWS, D_MODEL), row),
            _member((1, D_MODEL), (layer, pre)),
            _member((D_MODEL, D_FF), (layer, which)),
            _member((D_MODEL, D_FF), (layer, which)),
            _member((D_FF, D_MODEL), (layer, which)),
            _member((1, D_MODEL), (layer, post)),
        ],
        out_specs=pl.BlockSpec((FFN_ROWS, D_MODEL), row),
        scratch_shapes=[pltpu.VMEM((FFN_ROWS, D_MODEL), F32)],
        compiler_params=_params("parallel"),
        name="ffn",
    )(x2d, gains, w_gate, w_up, w_down, gains)


def _qkv_chunk(h, w_ref, grp, c):
    first = grp * GROUP_COLS + c * PROJ_COLS
    res = _dot(h, w_ref[:, first:first + PROJ_COLS])
    if c * PROJ_COLS < D_A:
        res = res * (DH_A ** -0.5)
    return res.astype(BF16)


def _attn_proj_kernel(x_ref, g_ref, w_ref, o0_ref, o1_ref, o2_ref, slab_ref):
    rows = x_ref.shape[1]
    h = _rms(x_ref[0], g_ref[...])
    for j in range(LANE_SLABS):
        slab_ref[j] = h[:, j * V7X_LANES:(j + 1) * V7X_LANES]
    hb = h.astype(BF16)
    for c in range(GROUP_COLS // PROJ_COLS):
        cols = slice(c * PROJ_COLS, (c + 1) * PROJ_COLS)
        o0_ref[0, 0, :, cols] = _qkv_chunk(hb, w_ref, 0, c)
    for grp, o_ref in ((1, o1_ref), (2, o2_ref)):
        dil = DILATED_GROUPS[grp][1]
        n = rows // dil
        by_residue = [
            jnp.concatenate([slab_ref[j, pl.ds(r, n, stride=dil), :] for j in range(LANE_SLABS)], axis=-1)
            for r in range(dil)]
        hp = jnp.concatenate(by_residue, axis=0).astype(BF16)
        for c in range(GROUP_COLS // PROJ_COLS):
            cols = slice(c * PROJ_COLS, (c + 1) * PROJ_COLS)
            res = _qkv_chunk(hp, w_ref, grp, c)
            for r in range(dil):
                o_ref[0, r, :, cols] = res[r * n:(r + 1) * n, :]


def _attn_proj(x, gains, w_in, layer):
    b, s, _ = x.shape
    shapes, specs = [], []
    for _, dil in DILATED_GROUPS:
        shapes.append(jax.ShapeDtypeStruct((b, dil, s // dil, GROUP_COLS), BF16))
        specs.append(pl.BlockSpec((1, dil, PROJ_ROWS // dil, GROUP_COLS), lambda bi, i: (bi, 0, i, 0)))
    return pl.pallas_call(
        _attn_proj_kernel,
        out_shape=tuple(shapes),
        grid=(b, s // PROJ_ROWS),
        in_specs=[
            pl.BlockSpec((1, PROJ_ROWS, D_MODEL), lambda bi, i: (bi, i, 0)),
            _member((1, D_MODEL), (layer, MIXER_PRE_NORM)),
            _member((D_MODEL, ATTN_IN_COLS), (layer // 2,)),
        ],
        out_specs=tuple(specs),
        scratch_shapes=[pltpu.VMEM((LANE_SLABS, PROJ_ROWS, V7X_LANES), F32)],
        compiler_params=_params("parallel", "parallel"),
        name="attn_proj",
    )(x, gains, w_in)


def _t5_buckets(rel):
    half = NUM_BUCKETS // 2
    max_exact = half // 2
    n = np.abs(rel)
    large = max_exact + (np.log(np.maximum(n, 1) / max_exact)
                         / np.log(REL_MAX_DISTANCE / max_exact) * (half - max_exact)).astype(np.int64)
    large = np.minimum(large, half - 1)
    return ((rel > 0) * half + np.where(n < max_exact, n, large)).astype(np.int32)


def _attn_window(ls):
    return min(ls, ATTN_Q + 2 * ATTN_RADIUS)


def _window_shifts(ls):
    w = _attn_window(ls)
    if w == ls:
        return (0,)
    return (0, -ATTN_RADIUS, ATTN_Q - w)


def _bias_tiles(rel_bias, g, dilation, ls):
    w = _attn_window(ls)
    n = w + ATTN_Q
    buckets = _t5_buckets(np.arange(-ATTN_RADIUS, ATTN_RADIUS + 1) * dilation)
    bias_g = rel_bias[g * H_A:(g + 1) * H_A][:, buckets].astype(F32)
    m = np.arange(n)
    key_minus_query = np.where(m < w, m, m - n)
    tiles = []
    for shift in _window_shifts(ls):
        off = shift + key_minus_query
        valid = np.abs(off) <= ATTN_RADIUS
        idx = np.clip(off + ATTN_RADIUS, 0, 2 * ATTN_RADIUS)
        band = jnp.where(valid[None], bias_g[:, idx], NEG_INF)
        flat = jnp.tile(band, (1, ATTN_Q))[:, :ATTN_Q * (n - 1)]
        tiles.append(flat.reshape(H_A, ATTN_Q, n - 1)[:, :, :w])
    return jnp.stack(tiles, axis=0)


def _stat_lane(pair, sub):
    return DH_A + pair if sub == 0 else pair


def _attn_kernel(q_ref, k_ref, v_ref, bias_ref, o_ref, m_ref, l_ref, *, ls, residues, blocks):
    w = _attn_window(ls)
    n_blocks = ls // ATTN_Q
    i = pl.program_id(2)
    subs = V7X_LANES // DH_A
    heads = [(pair, sub) for pair in range(HEAD_PAIRS) for sub in range(subs)]
    q_lane = lax.broadcasted_iota(jnp.int32, (ATTN_Q, V7X_LANES), 1)
    v_lane = lax.broadcasted_iota(jnp.int32, (w, V7X_LANES), 1)
    q_half = [q_lane < DH_A, q_lane >= DH_A]
    v_half = [v_lane < DH_A, v_lane >= DH_A]

    def window(u):
        if w == ls:
            return 0, 0
        blk = i * blocks + u
        start = pl.multiple_of(jnp.clip(blk * ATTN_Q - ATTN_RADIUS, 0, ls - w), ATTN_RADIUS)
        return start, jnp.where(blk == 0, 0, jnp.where(blk == n_blocks - 1, 2, 1))

    def scores_of(r, u):
        start, variant = window(u)
        q_rows = slice(u * ATTN_Q, (u + 1) * ATTN_Q)
        scores = {}
        for pair, sub in heads:
            cols = slice(pair * V7X_LANES, (pair + 1) * V7X_LANES)
            qp = q_ref[0, r, q_rows, cols]
            qh = jnp.where(q_half[sub], qp, jnp.zeros_like(qp))
            kp = k_ref[0, r, pl.ds(start, w), cols]
            scores[pair, sub] = _dot_nt(qh, kp) + bias_ref[variant, pair * subs + sub]
        return scores

    def finish(r, u, scores):
        start, _ = window(u)
        q_rows = slice(u * ATTN_Q, (u + 1) * ATTN_Q)
        maxima, probs = {}, {}
        for head, s in scores.items():
            maxima[head] = jnp.max(s, axis=-1, keepdims=True)
            probs[head] = jnp.exp(s - maxima[head]).astype(BF16)
        m_tile = jnp.zeros((ATTN_Q, V7X_LANES), F32)
        l_tile = jnp.ones((ATTN_Q, V7X_LANES), F32)
        for pair in range(HEAD_PAIRS):
            cols = slice(pair * V7X_LANES, (pair + 1) * V7X_LANES)
            vp = v_ref[0, r, pl.ds(start, w), cols]
            out = None
            for sub in range(subs):
                vh = jnp.where(v_half[sub], vp, jnp.ones_like(vp))
                pv = _dot(probs[pair, sub], vh)
                here = q_lane == _stat_lane(pair, sub)
                l_tile = jnp.where(here, pv, l_tile)
                m_tile = jnp.where(here, maxima[pair, sub], m_tile)
                out = pv if out is None else jnp.where(q_half[sub], pv, out)
            o_ref[0, r, q_rows, cols] = out.astype(BF16)
        m_ref[0, r, q_rows, :] = m_tile
        l_ref[0, r, q_rows, :] = l_tile

    units = [(r, u) for r in range(residues) for u in range(blocks)]
    pending = scores_of(*units[0])
    for idx, unit in enumerate(units):
        following = scores_of(*units[idx + 1]) if idx + 1 < len(units) else None
        finish(*unit, pending)
        pending = following


def _attn_group(qkv, bias_tiles):
    b, dil, ls, _ = qkv.shape
    w = _attn_window(ls)
    nvar = bias_tiles.shape[0]
    blocks = min(ATTN_UNITS, ls // ATTN_Q)
    residues = ATTN_UNITS // blocks
    rows = blocks * ATTN_Q
    blk = lambda bi, r, i: (bi, r, i, 0)
    return pl.pallas_call(
        functools.partial(_attn_kernel, ls=ls, residues=residues, blocks=blocks),
        out_shape=(jax.ShapeDtypeStruct((b, dil, ls, D_A), BF16),
                   jax.ShapeDtypeStruct((b, dil, ls, V7X_LANES), F32),
                   jax.ShapeDtypeStruct((b, dil, ls, V7X_LANES), F32)),
        grid=(b, dil // residues, ls // rows),
        in_specs=[
            pl.BlockSpec((1, residues, rows, D_A), blk),
            pl.BlockSpec((1, residues, ls, D_A), lambda bi, r, i: (bi, r, 0, 1)),
            pl.BlockSpec((1, residues, ls, D_A), lambda bi, r, i: (bi, r, 0, 2)),
            _resident((nvar, H_A, ATTN_Q, w), lambda bi, r, i: (0, 0, 0, 0)),
        ],
        out_specs=(pl.BlockSpec((1, residues, rows, D_A), blk),
                   pl.BlockSpec((1, residues, rows, V7X_LANES), blk),
                   pl.BlockSpec((1, residues, rows, V7X_LANES), blk)),
        compiler_params=_params("parallel", "parallel", "arbitrary"),
        name=f"attn_group_d{dil}",
    )(qkv, qkv, qkv, bias_tiles)


def _attn_out_kernel(x_ref, o0_ref, o1_ref, o2_ref, m0_ref, m1_ref, m2_ref, l0_ref, l1_ref, l2_ref,
                     expand_ref, w_ref, g_ref, out_ref, onat_ref, snat_ref):
    rows = x_ref.shape[1]
    for slot, (o_ref, m_ref, l_ref) in enumerate(((o1_ref, m1_ref, l1_ref), (o2_ref, m2_ref, l2_ref))):
        dil = o_ref.shape[1]
        n = rows // dil
        for r in range(dil):
            token_rows = pl.ds(r, n, stride=dil)
            snat_ref[slot, 0, token_rows, :] = m_ref[0, r]
            snat_ref[slot, 1, token_rows, :] = l_ref[0, r]
            blk = o_ref[0, r].astype(F32)
            for j in range(LANE_SLABS):
                onat_ref[slot, j, token_rows, :] = blk[:, j * V7X_LANES:(j + 1) * V7X_LANES]
    maxima = [m0_ref[0, 0], snat_ref[0, 0], snat_ref[1, 0]]
    sums = [l0_ref[0, 0], snat_ref[0, 1], snat_ref[1, 1]]
    outs = [o0_ref[0, 0].astype(F32)] + [
        jnp.concatenate([onat_ref[slot, j] for j in range(LANE_SLABS)], axis=-1) for slot in range(2)]
    top = jnp.maximum(jnp.maximum(maxima[0], maxima[1]), maxima[2])
    scale = [jnp.exp(m - top) for m in maxima]
    denom = scale[0] * sums[0] + scale[1] * sums[1] + scale[2] * sums[2]
    merged = None
    for sc, o in zip(scale, outs):
        wt = (sc / denom).astype(BF16)
        wt_wide = _dot(wt, expand_ref[...])
        term = wt_wide * o
        merged = term if merged is None else merged + term
    y = _dot(merged.astype(BF16), w_ref[...])
    out_ref[0] = x_ref[0] + _rms(y, g_ref[...])


def _attn_out(x, outs, maxima, sums, w_out, gains, layer):
    b, s, _ = x.shape
    head_of_col = np.arange(D_A) // DH_A
    lane_of_col = np.array([_stat_lane(h // 2, h % 2) for h in head_of_col])
    expand = jnp.asarray(np.arange(V7X_LANES)[:, None] == lane_of_col[None, :], BF16)
    tok = lambda bi, i: (bi, i, 0)
    grp = lambda bi, i: (bi, 0, i, 0)
    fixed = lambda bi, i: (0, 0)
    dils = [dil for _, dil in DILATED_GROUPS]
    stat_specs = [pl.BlockSpec((1, dil, OUT_ROWS // dil, V7X_LANES), grp) for dil in dils]
    return pl.pallas_call(
        _attn_out_kernel,
        out_shape=jax.ShapeDtypeStruct((b, s, D_MODEL), F32),
        grid=(b, s // OUT_ROWS),
        in_specs=[pl.BlockSpec((1, OUT_ROWS, D_MODEL), tok)]
        + [pl.BlockSpec((1, dil, OUT_ROWS // dil, D_A), grp) for dil in dils]
        + stat_specs + stat_specs
        + [_resident((V7X_LANES, D_A), fixed),
           _member((D_A, D_MODEL), (layer // 2,)),
           _member((1, D_MODEL), (layer, MIXER_POST_NORM))],
        out_specs=pl.BlockSpec((1, OUT_ROWS, D_MODEL), tok),
        scratch_shapes=[pltpu.VMEM((2, LANE_SLABS, OUT_ROWS, V7X_LANES), F32),
                        pltpu.VMEM((2, 2, OUT_ROWS, V7X_LANES), F32)],
        compiler_params=_params("parallel", "parallel"),
        name="attn_out",
    )(x, *outs, *maxima, *sums, expand, w_out, gains)


def _rotary(t, cos, sin):
    half = DK_R // 2
    parts = []
    for h in range(t.shape[1] // DK_R):
        t1 = t[:, h * DK_R:h * DK_R + half]
        t2 = t[:, h * DK_R + half:(h + 1) * DK_R]
        parts += [t1 * cos - t2 * sin, t1 * sin + t2 * cos]
    return jnp.concatenate(parts, axis=-1)


def _ret_proj_kernel(x_ref, g_ref, w_ref, cos_ref, sin_ref, q_ref, k_ref, v_ref, gf_ref, gb_ref):
    h = _rms(x_ref[0], g_ref[...]).astype(BF16)
    blocks = ((q_ref, 0, DK_R), (k_ref, RET_QK_COLS, DK_R), (v_ref, 2 * RET_QK_COLS, DV_R),
              (gf_ref, 2 * RET_QK_COLS + D_V, DV_R), (gb_ref, 2 * RET_QK_COLS + 2 * D_V, DV_R))
    for o_ref, start, width in blocks:
        for c in range(H_R * width // PROJ_COLS):
            first = start + c * PROJ_COLS
            res = _dot(h, w_ref[:, first:first + PROJ_COLS])
            if o_ref is q_ref or o_ref is k_ref:
                res = _rotary(res, cos_ref[...], sin_ref[...])
            if o_ref is k_ref:
                res = res * (DK_R ** -0.5)
            res = res.astype(BF16)
            for j in range(PROJ_COLS // width):
                o_ref[0, c * (PROJ_COLS // width) + j] = res[:, j * width:(j + 1) * width]


def _ret_proj(x, gains, w_in, cos, sin, layer):
    b, s, _ = x.shape
    assert RET_QK_COLS % PROJ_COLS == 0 and D_V % PROJ_COLS == 0 and PROJ_COLS % DV_R == 0
    pos = lambda bi, i: (i, 0)
    head_major = lambda bi, i: (bi, 0, i, 0)
    widths = (DK_R, DK_R, DV_R, DV_R, DV_R)
    return pl.pallas_call(
        _ret_proj_kernel,
        out_shape=tuple(jax.ShapeDtypeStruct((b, H_R, s, w), BF16) for w in widths),
        grid=(b, s // PROJ_ROWS),
        in_specs=[
            pl.BlockSpec((1, PROJ_ROWS, D_MODEL), lambda bi, i: (bi, i, 0)),
            _member((1, D_MODEL), (layer, MIXER_PRE_NORM)),
            _member((D_MODEL, RET_IN_COLS), (layer // 2,)),
            pl.BlockSpec((PROJ_ROWS, DK_R // 2), pos),
            pl.BlockSpec((PROJ_ROWS, DK_R // 2), pos),
        ],
        out_specs=tuple(pl.BlockSpec((1, H_R, PROJ_ROWS, w), head_major) for w in widths),
        compiler_params=_params("parallel", "parallel"),
        name="ret_proj",
    )(x, gains, w_in, cos, sin)


def _group_norm(y):
    mu = jnp.mean(y, axis=-1, keepdims=True)
    d = y - mu
    var = jnp.mean(d * d, axis=-1, keepdims=True)
    return d * lax.rsqrt(var + NORM_EPS)


def _retention_kernel(lg_ref, q_ref, k_ref, v_ref, gf_ref, gb_ref, o_ref, state_ref, y_ref):
    c_len = RET_SCAN_CHUNK
    n_chunks = q_ref.shape[2] // c_len
    head = pl.program_id(1)
    ii = lax.broadcasted_iota(jnp.int32, (c_len, c_len), 0).astype(F32)
    jj = lax.broadcasted_iota(jnp.int32, (c_len, c_len), 1).astype(F32)
    col = lax.broadcasted_iota(jnp.int32, (c_len, 1), 0).astype(F32)
    dirs = (0, 1)
    gate_refs = (gf_ref, gb_ref)
    dmat, q_decay, k_decay, chunk_decay = [], [], [], []
    for d in dirs:
        lg = lg_ref[d, head]
        rel = ii - jj if d == 0 else jj - ii
        dmat.append(jnp.where(rel >= 0, jnp.exp(lg * jnp.maximum(rel, 0.0)), 0.0))
        q_decay.append(jnp.exp(lg * (col + 1.0 if d == 0 else c_len - col)))
        k_decay.append(jnp.exp(lg * (c_len - 1.0 - col if d == 0 else col)))
        chunk_decay.append(jnp.exp(lg * c_len))
    state_ref[...] = jnp.zeros_like(state_ref)

    def step(n, finish):
        rows = [pl.ds(pl.multiple_of(chunk * c_len, c_len), c_len) for chunk in (n, n_chunks - 1 - n)]
        q = [q_ref[0, 0, rows[d], :] for d in dirs]
        k = [k_ref[0, 0, rows[d], :] for d in dirs]
        v = [v_ref[0, 0, rows[d], :] for d in dirs]
        state = [state_ref[d] for d in dirs]
        scores = [_dot_nt(q[d], k[d]) for d in dirs]
        cross = [_dot(q[d], state[d].astype(BF16)) for d in dirs]
        probs = [(scores[d] * dmat[d]).astype(BF16) for d in dirs]
        kd = [(k[d].astype(F32) * k_decay[d]).astype(BF16) for d in dirs]
        inner = [_dot(probs[d], v[d]) for d in dirs]
        update = [_dot_tn(kd[d], v[d]) for d in dirs]
        for d in dirs:
            state_ref[d] = state[d] * chunk_decay[d] + update[d]
        for d in dirs:
            gate = gate_refs[d][0, 0, rows[d], :].astype(F32)
            y = gate * jax.nn.sigmoid(gate) * _group_norm(inner[d] + cross[d] * q_decay[d])
            if finish:
                o_ref[0, rows[d], :] = (y + y_ref[1 - d, rows[d], :]).astype(BF16)
            else:
                y_ref[d, rows[d], :] = y

    def first_half(n, carry):
        step(n, False)
        return carry

    def second_half(n, carry):
        step(n, True)
        return carry

    lax.fori_loop(0, n_chunks // 2, first_half, 0, unroll=RET_UNROLL)
    lax.fori_loop(n_chunks // 2, n_chunks, second_half, 0, unroll=RET_UNROLL)


def _retention(q, k, v, gate_f, gate_b, log_gamma):
    b, _, s, _ = q.shape
    assert s % (2 * RET_UNROLL * RET_SCAN_CHUNK) == 0
    head = lambda bi, h, lg: (bi, h, 0, 0)
    return pl.pallas_call(
        _retention_kernel,
        out_shape=jax.ShapeDtypeStruct((b, s, D_V), BF16),
        grid_spec=pltpu.PrefetchScalarGridSpec(
            num_scalar_prefetch=1,
            grid=(b, H_R),
            in_specs=[pl.BlockSpec((1, 1, s, DK_R), head)] * 2 + [pl.BlockSpec((1, 1, s, DV_R), head)] * 3,
            out_specs=pl.BlockSpec((1, s, DV_R), lambda bi, h, lg: (bi, 0, h)),
            scratch_shapes=[pltpu.VMEM((2, DK_R, DV_R), F32), pltpu.VMEM((2, s, DV_R), F32)],
        ),
        compiler_params=_params("parallel", "parallel"),
        name="retention",
    )(log_gamma, q, k, v, gate_f, gate_b)


def _proj_out_kernel(x_ref, y_ref, w_ref, g_ref, out_ref):
    out_ref[...] = x_ref[...] + _rms(_dot(y_ref[...], w_ref[...]), g_ref[...])


def _proj_out(x2d, y2d, w_out, gains, layer):
    t = x2d.shape[0]
    k = y2d.shape[1]
    row = lambda i: (i, 0)
    return pl.pallas_call(
        _proj_out_kernel,
        out_shape=jax.ShapeDtypeStruct((t, D_MODEL), F32),
        grid=(t // RET_OUT_ROWS,),
        in_specs=[
            pl.BlockSpec((RET_OUT_ROWS, D_MODEL), row),
            pl.BlockSpec((RET_OUT_ROWS, k), row),
            _member((k, D_MODEL), (layer // 2,)),
            _member((1, D_MODEL), (layer, MIXER_POST_NORM)),
        ],
        out_specs=pl.BlockSpec((RET_OUT_ROWS, D_MODEL), row),
        compiler_params=_params("parallel"),
        name="proj_out",
    )(x2d, y2d, w_out, gains)


def _rope_tables(s):
    half = DK_R // 2
    inv_freq = 1.0 / (ROPE_BASE ** jnp.linspace(0.0, 1.0, half, dtype=F32))
    ang = jnp.arange(s, dtype=F32)[:, None] * inv_freq[None, :]
    return jnp.cos(ang), jnp.sin(ang)


def kernel(x, norm_gains, ffn_w_gate, ffn_w_up, ffn_w_down, attn_w_in, attn_w_out,
           rel_bias, ret_w_in, ret_w_out, ret_decay_logit):
    b, s, d = x.shape
    t = b * s
    gains = norm_gains.astype(F32).reshape(DEPTH, -1, 1, D_MODEL)
    w_gate = ffn_w_gate.astype(BF16)
    w_up = ffn_w_up.astype(BF16)
    w_down = ffn_w_down.astype(BF16)
    a_w_in = attn_w_in.astype(BF16)
    a_w_out = attn_w_out.astype(BF16)
    r_w_in = ret_w_in.astype(BF16)
    r_w_out = ret_w_out.astype(BF16)
    bias_tiles = [_bias_tiles(rel_bias, g, dil, s // dil)
                  for g, (_, dil) in enumerate(DILATED_GROUPS)]
    cos, sin = _rope_tables(s)
    log_gamma = jnp.log1p(-jnp.exp(ret_decay_logit.astype(F32)))

    x2d = x.reshape(t, d)
    for i in range(DEPTH):
        x2d = _ffn(x2d, gains, w_gate, w_up, w_down, i, 0)
        if i % 2 == 0:
            x3d = x2d.reshape(b, s, d)
            qkvs = _attn_proj(x3d, gains, a_w_in, i)
            outs, maxima, sums = zip(*[_attn_group(qkv, bt) for qkv, bt in zip(qkvs, bias_tiles)])
            x2d = _attn_out(x3d, outs, maxima, sums, a_w_out, gains, i).reshape(t, d)
        else:
            heads = _ret_proj(x2d.reshape(b, s, d), gains, r_w_in, cos, sin, i)
            y = _retention(*heads, log_gamma[i // 2])
            x2d = _proj_out(x2d, y.reshape(t, D_V), r_w_out, gains, i)
        x2d = _ffn(x2d, gains, w_gate, w_up, w_down, i, 1)
    return x2d.reshape(b, s, d)
```

```python
import functools

import jax
import jax.numpy as jnp
import numpy as np
from jax import lax
from jax.experimental import pallas as pl
from jax.experimental.pallas import tpu as pltpu

D_MODEL = 1024
DEPTH = 4
NORM_EPS = 1e-6
FFN_NORMS = ((0, 1), (4, 5))
MIXER_PRE_NORM, MIXER_POST_NORM = 2, 3
D_FF = 2816
DILATED_GROUPS = ((128, 1), (512, 4), (2048, 16))
N_GROUPS = len(DILATED_GROUPS)
H_A = 16
DH_A = 64
D_A = H_A * DH_A
GROUP_COLS = 3 * D_A
ATTN_IN_COLS = N_GROUPS * GROUP_COLS
NEG_INF = -1e30
NUM_BUCKETS = 32
REL_MAX_DISTANCE = 1024
H_R = 4
DK_R = D_MODEL // H_R
DV_R = 2 * D_MODEL // H_R
D_V = H_R * DV_R
RET_QK_COLS = H_R * DK_R
RET_IN_COLS = 2 * RET_QK_COLS + 3 * D_V
ROPE_BASE = 10000.0

V7X_LANES = 128
V7X_VMEM_LIMIT_BYTES = 56 * 1024 * 1024

FFN_ROWS = 1024
FFN_COLS = 256
PROJ_ROWS = 512
PROJ_COLS = 1024
ATTN_Q = 128
ATTN_RADIUS = 64
ATTN_UNITS = 8
RET_SCAN_CHUNK = 256
OUT_ROWS = 1024
RET_OUT_ROWS = 1024
RET_UNROLL = 4
LANE_SLABS = D_MODEL // V7X_LANES
HEAD_PAIRS = D_A // V7X_LANES

BF16 = jnp.bfloat16
F32 = jnp.float32


def _params(*semantics):
    return pltpu.CompilerParams(dimension_semantics=semantics,
                                vmem_limit_bytes=V7X_VMEM_LIMIT_BYTES)


def _resident(shape, index_map):
    return pl.BlockSpec(shape, index_map, pipeline_mode=pl.Buffered(1))


def _member(shape, lead):
    index = tuple(lead) + (0,) * len(shape)
    return _resident((None,) * len(lead) + tuple(shape), lambda *_: index)


def _rms(x, g):
    ms = jnp.mean(x * x, axis=-1, keepdims=True)
    return x * lax.rsqrt(ms + NORM_EPS) * g


def _dot(a, b):
    return jnp.dot(a, b, preferred_element_type=F32)


def _dot_nt(a, b):
    return lax.dot_general(a, b, (((1,), (1,)), ((), ())), preferred_element_type=F32)


def _dot_tn(a, b):
    return lax.dot_general(a, b, (((0,), (0,)), ((), ())), preferred_element_type=F32)


def _ffn_kernel(x_ref, gpre_ref, wg_ref, wu_ref, wd_ref, gpost_ref, o_ref, acc_ref):
    x = x_ref[...]
    h = _rms(x, gpre_ref[...]).astype(BF16)
    for c in range(D_FF // FFN_COLS):
        cols = slice(c * FFN_COLS, (c + 1) * FFN_COLS)
        gate = _dot(h, wg_ref[:, cols])
        up = _dot(h, wu_ref[:, cols])
        act = (gate * jax.nn.sigmoid(gate) * up).astype(BF16)
        part = _dot(act, wd_ref[cols, :])
        if c == 0:
            acc_ref[...] = part
        else:
            acc_ref[...] += part
    o_ref[...] = x + 0.5 * _rms(acc_ref[...], gpost_ref[...])


def _ffn(x2d, gains, w_gate, w_up, w_down, layer, which):
    t = x2d.shape[0]
    row = lambda i: (i, 0)
    pre, post = FFN_NORMS[which]
    return pl.pallas_call(
        _ffn_kernel,
        out_shape=jax.ShapeDtypeStruct((t, D_MODEL), F32),
        grid=(t // FFN_ROWS,),
        in_specs=[
            pl.BlockSpec((FFN_ROWS, D_MODEL), row),
            _member((1, D_MODEL), (layer, pre)),
            _member((D_MODEL, D_FF), (layer, which)),
            _member((D_MODEL, D_FF), (layer, which)),
            _member((D_FF, D_MODEL), (layer, which)),
            _member((1, D_MODEL), (layer, post)),
        ],
        out_specs=pl.BlockSpec((FFN_ROWS, D_MODEL), row),
        scratch_shapes=[pltpu.VMEM((FFN_ROWS, D_MODEL), F32)],
        compiler_params=_params("parallel"),
        name="ffn",
    )(x2d, gains, w_gate, w_up, w_down, gains)


def _qkv_chunk(h, w_ref, grp, c):
    first = grp * GROUP_COLS + c * PROJ_COLS
    res = _dot(h, w_ref[:, first:first + PROJ_COLS])
    if c * PROJ_COLS < D_A:
        res = res * (DH_A ** -0.5)
    return res.astype(BF16)


def _attn_proj_kernel(x_ref, g_ref, w_ref, o0_ref, o1_ref, o2_ref, slab_ref):
    rows = x_ref.shape[1]
    h = _rms(x_ref[0], g_ref[...])
    for j in range(LANE_SLABS):
        slab_ref[j] = h[:, j * V7X_LANES:(j + 1) * V7X_LANES]
    hb = h.astype(BF16)
    for c in range(GROUP_COLS // PROJ_COLS):
        cols = slice(c * PROJ_COLS, (c + 1) * PROJ_COLS)
        o0_ref[0, 0, :, cols] = _qkv_chunk(hb, w_ref, 0, c)
    for grp, o_ref in ((1, o1_ref), (2, o2_ref)):
        dil = DILATED_GROUPS[grp][1]
        n = rows // dil
        by_residue = [
            jnp.concatenate([slab_ref[j, pl.ds(r, n, stride=dil), :] for j in range(LANE_SLABS)], axis=-1)
            for r in range(dil)]
        hp = jnp.concatenate(by_residue, axis=0).astype(BF16)
        for c in range(GROUP_COLS // PROJ_COLS):
            cols = slice(c * PROJ_COLS, (c + 1) * PROJ_COLS)
            res = _qkv_chunk(hp, w_ref, grp, c)
            for r in range(dil):
                o_ref[0, r, :, cols] = res[r * n:(r + 1) * n, :]


def _attn_proj(x, gains, w_in, layer):
    b, s, _ = x.shape
    shapes, specs = [], []
    for _, dil in DILATED_GROUPS:
        shapes.append(jax.ShapeDtypeStruct((b, dil, s // dil, GROUP_COLS), BF16))
        specs.append(pl.BlockSpec((1, dil, PROJ_ROWS // dil, GROUP_COLS), lambda bi, i: (bi, 0, i, 0)))
    return pl.pallas_call(
        _attn_proj_kernel,
        out_shape=tuple(shapes),
        grid=(b, s // PROJ_ROWS),
        in_specs=[
            pl.BlockSpec((1, PROJ_ROWS, D_MODEL), lambda bi, i: (bi, i, 0)),
            _member((1, D_MODEL), (layer, MIXER_PRE_NORM)),
            _member((D_MODEL, ATTN_IN_COLS), (layer // 2,)),
        ],
        out_specs=tuple(specs),
        scratch_shapes=[pltpu.VMEM((LANE_SLABS, PROJ_ROWS, V7X_LANES), F32)],
        compiler_params=_params("parallel", "parallel"),
        name="attn_proj",
    )(x, gains, w_in)


def _t5_buckets(rel):
    half = NUM_BUCKETS // 2
    max_exact = half // 2
    n = np.abs(rel)
    large = max_exact + (np.log(np.maximum(n, 1) / max_exact)
                         / np.log(REL_MAX_DISTANCE / max_exact) * (half - max_exact)).astype(np.int64)
    large = np.minimum(large, half - 1)
    return ((rel > 0) * half + np.where(n < max_exact, n, large)).astype(np.int32)


def _attn_window(ls):
    return min(ls, ATTN_Q + 2 * ATTN_RADIUS)


def _window_shifts(ls):
    w = _attn_window(ls)
    if w == ls:
        return (0,)
    return (0, -ATTN_RADIUS, ATTN_Q - w)


def _bias_tiles(rel_bias, g, dilation, ls):
    w = _attn_window(ls)
    n = w + ATTN_Q
    buckets = _t5_buckets(np.arange(-ATTN_RADIUS, ATTN_RADIUS + 1) * dilation)
    bias_g = rel_bias[g * H_A:(g + 1) * H_A][:, buckets].astype(F32)
    m = np.arange(n)
    key_minus_query = np.where(m < w, m, m - n)
    tiles = []
    for shift in _window_shifts(ls):
        off = shift + key_minus_query
        valid = np.abs(off) <= ATTN_RADIUS
        idx = np.clip(off + ATTN_RADIUS, 0, 2 * ATTN_RADIUS)
        band = jnp.where(valid[None], bias_g[:, idx], NEG_INF)
        flat = jnp.tile(band, (1, ATTN_Q))[:, :ATTN_Q * (n - 1)]
        tiles.append(flat.reshape(H_A, ATTN_Q, n - 1)[:, :, :w])
    return jnp.stack(tiles, axis=0)


def _stat_lane(pair, sub):
    return DH_A + pair if sub == 0 else pair


def _attn_kernel(q_ref, k_ref, v_ref, bias_ref, o_ref, m_ref, l_ref, *, ls, residues, blocks):
    w = _attn_window(ls)
    n_blocks = ls // ATTN_Q
    i = pl.program_id(2)
    subs = V7X_LANES // DH_A
    heads = [(pair, sub) for pair in range(HEAD_PAIRS) for sub in range(subs)]
    q_lane = lax.broadcasted_iota(jnp.int32, (ATTN_Q, V7X_LANES), 1)
    v_lane = lax.broadcasted_iota(jnp.int32, (w, V7X_LANES), 1)
    q_half = [q_lane < DH_A, q_lane >= DH_A]
    v_half = [v_lane < DH_A, v_lane >= DH_A]

    def window(u):
        if w == ls:
            return 0, 0
        blk = i * blocks + u
        start = pl.multiple_of(jnp.clip(blk * ATTN_Q - ATTN_RADIUS, 0, ls - w), ATTN_RADIUS)
        return start, jnp.where(blk == 0, 0, jnp.where(blk == n_blocks - 1, 2, 1))

    def scores_of(r, u):
        start, variant = window(u)
        q_rows = slice(u * ATTN_Q, (u + 1) * ATTN_Q)
        scores = {}
        for pair, sub in heads:
            cols = slice(pair * V7X_LANES, (pair + 1) * V7X_LANES)
            qp = q_ref[0, r, q_rows, cols]
            qh = jnp.where(q_half[sub], qp, jnp.zeros_like(qp))
            kp = k_ref[0, r, pl.ds(start, w), cols]
            scores[pair, sub] = _dot_nt(qh, kp) + bias_ref[variant, pair * subs + sub]
        return scores

    def finish(r, u, scores):
        start, _ = window(u)
        q_rows = slice(u * ATTN_Q, (u + 1) * ATTN_Q)
        maxima, probs = {}, {}
        for head, s in scores.items():
            maxima[head] = jnp.max(s, axis=-1, keepdims=True)
            probs[head] = jnp.exp(s - maxima[head]).astype(BF16)
        m_tile = jnp.zeros((ATTN_Q, V7X_LANES), F32)
        l_tile = jnp.ones((ATTN_Q, V7X_LANES), F32)
        for pair in range(HEAD_PAIRS):
            cols = slice(pair * V7X_LANES, (pair + 1) * V7X_LANES)
            vp = v_ref[0, r, pl.ds(start, w), cols]
            out = None
            for sub in range(subs):
                vh = jnp.where(v_half[sub], vp, jnp.ones_like(vp))
                pv = _dot(probs[pair, sub], vh)
                here = q_lane == _stat_lane(pair, sub)
                l_tile = jnp.where(here, pv, l_tile)
                m_tile = jnp.where(here, maxima[pair, sub], m_tile)
                out = pv if out is None else jnp.where(q_half[sub], pv, out)
            o_ref[0, r, q_rows, cols] = out.astype(BF16)
        m_ref[0, r, q_rows, :] = m_tile
        l_ref[0, r, q_rows, :] = l_tile

    units = [(r, u) for r in range(residues) for u in range(blocks)]
    pending = scores_of(*units[0])
    for idx, unit in enumerate(units):
        following = scores_of(*units[idx + 1]) if idx + 1 < len(units) else None
        finish(*unit, pending)
        pending = following


def _attn_group(qkv, bias_tiles):
    b, dil, ls, _ = qkv.shape
    w = _attn_window(ls)
    nvar = bias_tiles.shape[0]
    blocks = min(ATTN_UNITS, ls // ATTN_Q)
    residues = ATTN_UNITS // blocks
    rows = blocks * ATTN_Q
    blk = lambda bi, r, i: (bi, r, i, 0)
    return pl.pallas_call(
        functools.partial(_attn_kernel, ls=ls, residues=residues, blocks=blocks),
        out_shape=(jax.ShapeDtypeStruct((b, dil, ls, D_A), BF16),
                   jax.ShapeDtypeStruct((b, dil, ls, V7X_LANES), F32),
                   jax.ShapeDtypeStruct((b, dil, ls, V7X_LANES), F32)),
        grid=(b, dil // residues, ls // rows),
        in_specs=[
            pl.BlockSpec((1, residues, rows, D_A), blk),
            pl.BlockSpec((1, residues, ls, D_A), lambda bi, r, i: (bi, r, 0, 1)),
            pl.BlockSpec((1, residues, ls, D_A), lambda bi, r, i: (bi, r, 0, 2)),
            _resident((nvar, H_A, ATTN_Q, w), lambda bi, r, i: (0, 0, 0, 0)),
        ],
        out_specs=(pl.BlockSpec((1, residues, rows, D_A), blk),
                   pl.BlockSpec((1, residues, rows, V7X_LANES), blk),
                   pl.BlockSpec((1, residues, rows, V7X_LANES), blk)),
        compiler_params=_params("parallel", "parallel", "arbitrary"),
        name=f"attn_group_d{dil}",
    )(qkv, qkv, qkv, bias_tiles)


def _attn_out_kernel(x_ref, o0_ref, o1_ref, o2_ref, m0_ref, m1_ref, m2_ref, l0_ref, l1_ref, l2_ref,
                     expand_ref, w_ref, g_ref, out_ref, onat_ref, snat_ref):
    rows = x_ref.shape[1]
    for slot, (o_ref, m_ref, l_ref) in enumerate(((o1_ref, m1_ref, l1_ref), (o2_ref, m2_ref, l2_ref))):
        dil = o_ref.shape[1]
        n = rows // dil
        for r in range(dil):
            token_rows = pl.ds(r, n, stride=dil)
            snat_ref[slot, 0, token_rows, :] = m_ref[0, r]
            snat_ref[slot, 1, token_rows, :] = l_ref[0, r]
            blk = o_ref[0, r].astype(F32)
            for j in range(LANE_SLABS):
                onat_ref[slot, j, token_rows, :] = blk[:, j * V7X_LANES:(j + 1) * V7X_LANES]
    maxima = [m0_ref[0, 0], snat_ref[0, 0], snat_ref[1, 0]]
    sums = [l0_ref[0, 0], snat_ref[0, 1], snat_ref[1, 1]]
    outs = [o0_ref[0, 0].astype(F32)] + [
        jnp.concatenate([onat_ref[slot, j] for j in range(LANE_SLABS)], axis=-1) for slot in range(2)]
    top = jnp.maximum(jnp.maximum(maxima[0], maxima[1]), maxima[2])
    scale = [jnp.exp(m - top) for m in maxima]
    denom = scale[0] * sums[0] + scale[1] * sums[1] + scale[2] * sums[2]
    merged = None
    for sc, o in zip(scale, outs):
        wt = (sc / denom).astype(BF16)
        wt_wide = _dot(wt, expand_ref[...])
        term = wt_wide * o
        merged = term if merged is None else merged + term
    y = _dot(merged.astype(BF16), w_ref[...])
    out_ref[0] = x_ref[0] + _rms(y, g_ref[...])


def _attn_out(x, outs, maxima, sums, w_out, gains, layer):
    b, s, _ = x.shape
    head_of_col = np.arange(D_A) // DH_A
    lane_of_col = np.array([_stat_lane(h // 2, h % 2) for h in head_of_col])
    expand = jnp.asarray(np.arange(V7X_LANES)[:, None] == lane_of_col[None, :], BF16)
    tok = lambda bi, i: (bi, i, 0)
    grp = lambda bi, i: (bi, 0, i, 0)
    fixed = lambda bi, i: (0, 0)
    dils = [dil for _, dil in DILATED_GROUPS]
    stat_specs = [pl.BlockSpec((1, dil, OUT_ROWS // dil, V7X_LANES), grp) for dil in dils]
    return pl.pallas_call(
        _attn_out_kernel,
        out_shape=jax.ShapeDtypeStruct((b, s, D_MODEL), F32),
        grid=(b, s // OUT_ROWS),
        in_specs=[pl.BlockSpec((1, OUT_ROWS, D_MODEL), tok)]
        + [pl.BlockSpec((1, dil, OUT_ROWS // dil, D_A), grp) for dil in dils]
        + stat_specs + stat_specs
        + [_resident((V7X_LANES, D_A), fixed),
           _member((D_A, D_MODEL), (layer // 2,)),
           _member((1, D_MODEL), (layer, MIXER_POST_NORM))],
        out_specs=pl.BlockSpec((1, OUT_ROWS, D_MODEL), tok),
        scratch_shapes=[pltpu.VMEM((2, LANE_SLABS, OUT_ROWS, V7X_LANES), F32),
                        pltpu.VMEM((2, 2, OUT_ROWS, V7X_LANES), F32)],
        compiler_params=_params("parallel", "parallel"),
        name="attn_out",
    )(x, *outs, *maxima, *sums, expand, w_out, gains)


def _rotary(t, cos, sin):
    half = DK_R // 2
    parts = []
    for h in range(t.shape[1] // DK_R):
        t1 = t[:, h * DK_R:h * DK_R + half]
        t2 = t[:, h * DK_R + half:(h + 1) * DK_R]
        parts += [t1 * cos - t2 * sin, t1 * sin + t2 * cos]
    return jnp.concatenate(parts, axis=-1)


def _ret_proj_kernel(x_ref, g_ref, w_ref, cos_ref, sin_ref, o_ref):
    h = _rms(x_ref[...], g_ref[...]).astype(BF16)
    for c in range(RET_IN_COLS // PROJ_COLS):
        first = c * PROJ_COLS
        cols = slice(first, first + PROJ_COLS)
        res = _dot(h, w_ref[:, cols])
        if first < 2 * RET_QK_COLS:
            res = _rotary(res, cos_ref[...], sin_ref[...])
            if first >= RET_QK_COLS:
                res = res * (DK_R ** -0.5)
        o_ref[:, cols] = res.astype(BF16)


def _ret_proj(x2d, gains, w_in, cos, sin, s, layer):
    t = x2d.shape[0]
    assert RET_QK_COLS % PROJ_COLS == 0 and PROJ_COLS % DK_R == 0
    row = lambda i: (i, 0)
    pos = lambda i: (i % (s // PROJ_ROWS), 0)
    return pl.pallas_call(
        _ret_proj_kernel,
        out_shape=jax.ShapeDtypeStruct((t, RET_IN_COLS), BF16),
        grid=(t // PROJ_ROWS,),
        in_specs=[
            pl.BlockSpec((PROJ_ROWS, D_MODEL), row),
            _member((1, D_MODEL), (layer, MIXER_PRE_NORM)),
            _member((D_MODEL, RET_IN_COLS), (layer // 2,)),
            pl.BlockSpec((PROJ_ROWS, DK_R // 2), pos),
            pl.BlockSpec((PROJ_ROWS, DK_R // 2), pos),
        ],
        out_specs=pl.BlockSpec((PROJ_ROWS, RET_IN_COLS), row),
        compiler_params=_params("parallel"),
        name="ret_proj",
    )(x2d, gains, w_in, cos, sin)


def _group_norm(y):
    mu = jnp.mean(y, axis=-1, keepdims=True)
    d = y - mu
    var = jnp.mean(d * d, axis=-1, keepdims=True)
    return d * lax.rsqrt(var + NORM_EPS)


def _retention_kernel(lg_ref, q_ref, k_ref, v_ref, gf_ref, gb_ref, o_ref, state_ref, y_ref):
    c_len = RET_SCAN_CHUNK
    n_chunks = q_ref.shape[1] // c_len
    head = pl.program_id(1)
    ii = lax.broadcasted_iota(jnp.int32, (c_len, c_len), 0).astype(F32)
    jj = lax.broadcasted_iota(jnp.int32, (c_len, c_len), 1).astype(F32)
    col = lax.broadcasted_iota(jnp.int32, (c_len, 1), 0).astype(F32)
    dirs = (0, 1)
    gate_refs = (gf_ref, gb_ref)
    dmat, q_decay, k_decay, chunk_decay = [], [], [], []
    for d in dirs:
        lg = lg_ref[d, head]
        rel = ii - jj if d == 0 else jj - ii
        dmat.append(jnp.where(rel >= 0, jnp.exp(lg * jnp.maximum(rel, 0.0)), 0.0))
        q_decay.append(jnp.exp(lg * (col + 1.0 if d == 0 else c_len - col)))
        k_decay.append(jnp.exp(lg * (c_len - 1.0 - col if d == 0 else col)))
        chunk_decay.append(jnp.exp(lg * c_len))
    state_ref[...] = jnp.zeros_like(state_ref)

    def step(n, finish):
        rows = [pl.ds(pl.multiple_of(chunk * c_len, c_len), c_len) for chunk in (n, n_chunks - 1 - n)]
        q = [q_ref[0, rows[d], :] for d in dirs]
        k = [k_ref[0, rows[d], :] for d in dirs]
        v = [v_ref[0, rows[d], :] for d in dirs]
        state = [state_ref[d] for d in dirs]
        scores = [_dot_nt(q[d], k[d]) for d in dirs]
        cross = [_dot(q[d], state[d].astype(BF16)) for d in dirs]
        probs = [(scores[d] * dmat[d]).astype(BF16) for d in dirs]
        kd = [(k[d].astype(F32) * k_decay[d]).astype(BF16) for d in dirs]
        inner = [_dot(probs[d], v[d]) for d in dirs]
        update = [_dot_tn(kd[d], v[d]) for d in dirs]
        for d in dirs:
            state_ref[d] = state[d] * chunk_decay[d] + update[d]
        for d in dirs:
            gate = gate_refs[d][0, rows[d], :].astype(F32)
            y = gate * jax.nn.sigmoid(gate) * _group_norm(inner[d] + cross[d] * q_decay[d])
            if finish:
                o_ref[0, rows[d], :] = (y + y_ref[1 - d, rows[d], :]).astype(BF16)
            else:
                y_ref[d, rows[d], :] = y

    def first_half(n, carry):
        step(n, False)
        return carry

    def second_half(n, carry):
        step(n, True)
        return carry

    lax.fori_loop(0, n_chunks // 2, first_half, 0, unroll=RET_UNROLL)
    lax.fori_loop(n_chunks // 2, n_chunks, second_half, 0, unroll=RET_UNROLL)


def _retention(proj, log_gamma):
    b, s, _ = proj.shape
    assert s % (2 * RET_UNROLL * RET_SCAN_CHUNK) == 0
    q0 = 0
    k0 = H_R
    v0 = (2 * RET_QK_COLS) // DV_R
    gf0 = v0 + H_R
    gb0 = gf0 + H_R
    return pl.pallas_call(
        _retention_kernel,
        out_shape=jax.ShapeDtypeStruct((b, s, D_V), BF16),
        grid_spec=pltpu.PrefetchScalarGridSpec(
            num_scalar_prefetch=1,
            grid=(b, H_R),
            in_specs=[
                pl.BlockSpec((1, s, DK_R), lambda bi, h, lg: (bi, 0, q0 + h)),
                pl.BlockSpec((1, s, DK_R), lambda bi, h, lg: (bi, 0, k0 + h)),
                pl.BlockSpec((1, s, DV_R), lambda bi, h, lg: (bi, 0, v0 + h)),
                pl.BlockSpec((1, s, DV_R), lambda bi, h, lg: (bi, 0, gf0 + h)),
                pl.BlockSpec((1, s, DV_R), lambda bi, h, lg: (bi, 0, gb0 + h)),
            ],
            out_specs=pl.BlockSpec((1, s, DV_R), lambda bi, h, lg: (bi, 0, h)),
            scratch_shapes=[pltpu.VMEM((2, DK_R, DV_R), F32), pltpu.VMEM((2, s, DV_R), F32)],
        ),
        compiler_params=_params("parallel", "parallel"),
        name="retention",
    )(log_gamma, proj, proj, proj, proj, proj)


def _proj_out_kernel(x_ref, y_ref, w_ref, g_ref, out_ref):
    out_ref[...] = x_ref[...] + _rms(_dot(y_ref[...], w_ref[...]), g_ref[...])


def _proj_out(x2d, y2d, w_out, gains, layer):
    t = x2d.shape[0]
    k = y2d.shape[1]
    row = lambda i: (i, 0)
    return pl.pallas_call(
        _proj_out_kernel,
        out_shape=jax.ShapeDtypeStruct((t, D_MODEL), F32),
        grid=(t // RET_OUT_ROWS,),
        in_specs=[
            pl.BlockSpec((RET_OUT_ROWS, D_MODEL), row),
            pl.BlockSpec((RET_OUT_ROWS, k), row),
            _member((k, D_MODEL), (layer // 2,)),
            _member((1, D_MODEL), (layer, MIXER_POST_NORM)),
        ],
        out_specs=pl.BlockSpec((RET_OUT_ROWS, D_MODEL), row),
        compiler_params=_params("parallel"),
        name="proj_out",
    )(x2d, y2d, w_out, gains)


def _rope_tables(s):
    half = DK_R // 2
    inv_freq = 1.0 / (ROPE_BASE ** jnp.linspace(0.0, 1.0, half, dtype=F32))
    ang = jnp.arange(s, dtype=F32)[:, None] * inv_freq[None, :]
    return jnp.cos(ang), jnp.sin(ang)


def kernel(x, norm_gains, ffn_w_gate, ffn_w_up, ffn_w_down, attn_w_in, attn_w_out,
           rel_bias, ret_w_in, ret_w_out, ret_decay_logit):
    b, s, d = x.shape
    t = b * s
    gains = norm_gains.astype(F32).reshape(DEPTH, -1, 1, D_MODEL)
    w_gate = ffn_w_gate.astype(BF16)
    w_up = ffn_w_up.astype(BF16)
    w_down = ffn_w_down.astype(BF16)
    a_w_in = attn_w_in.astype(BF16)
    a_w_out = attn_w_out.astype(BF16)
    r_w_in = ret_w_in.astype(BF16)
    r_w_out = ret_w_out.astype(BF16)
    bias_tiles = [_bias_tiles(rel_bias, g, dil, s // dil)
                  for g, (_, dil) in enumerate(DILATED_GROUPS)]
    cos, sin = _rope_tables(s)
    log_gamma = jnp.log1p(-jnp.exp(ret_decay_logit.astype(F32)))

    x2d = x.reshape(t, d)
    for i in range(DEPTH):
        x2d = _ffn(x2d, gains, w_gate, w_up, w_down, i, 0)
        if i % 2 == 0:
            x3d = x2d.reshape(b, s, d)
            qkvs = _attn_proj(x3d, gains, a_w_in, i)
            outs, maxima, sums = zip(*[_attn_group(qkv, bt) for qkv, bt in zip(qkvs, bias_tiles)])
            x2d = _attn_out(x3d, outs, maxima, sums, a_w_out, gains, i).reshape(t, d)
        else:
            proj = _ret_proj(x2d, gains, r_w_in, cos, sin, s, i).reshape(b, s, RET_IN_COLS)
            y = _retention(proj, log_gamma[i // 2])
            x2d = _proj_out(x2d, y.reshape(t, D_V), r_w_out, gains, i)
        x2d = _ffn(x2d, gains, w_gate, w_up, w_down, i, 1)
    return x2d.reshape(b, s, d)
```

```python
import functools

import jax
import jax.numpy as jnp
import numpy as np
from jax import lax
from jax.experimental import pallas as pl
from jax.experimental.pallas import tpu as pltpu

D_MODEL = 1024
DEPTH = 4
NORM_EPS = 1e-6
FFN_NORMS = ((0, 1), (4, 5))
MIXER_PRE_NORM, MIXER_POST_NORM = 2, 3
D_FF = 2816
DILATED_GROUPS = ((128, 1), (512, 4), (2048, 16))
N_GROUPS = len(DILATED_GROUPS)
H_A = 16
DH_A = 64
D_A = H_A * DH_A
GROUP_COLS = 3 * D_A
ATTN_IN_COLS = N_GROUPS * GROUP_COLS
NEG_INF = -1e30
NUM_BUCKETS = 32
REL_MAX_DISTANCE = 1024
H_R = 4
DK_R = D_MODEL // H_R
DV_R = 2 * D_MODEL // H_R
D_V = H_R * DV_R
RET_QK_COLS = H_R * DK_R
RET_IN_COLS = 2 * RET_QK_COLS + 3 * D_V
ROPE_BASE = 10000.0

V7X_LANES = 128
V7X_VMEM_LIMIT_BYTES = 56 * 1024 * 1024

FFN_ROWS = 1024
FFN_COLS = 256
FFN_PAIR_ROWS = 512
PROJ_ROWS = 512
PROJ_COLS = 1024
ATTN_Q = 128
ATTN_RADIUS = 64
ATTN_UNITS = 8
RET_SCAN_CHUNK = 256
OUT_ROWS = 1024
RET_OUT_ROWS = 1024
RET_UNROLL = 4
LANE_SLABS = D_MODEL // V7X_LANES
HEAD_PAIRS = D_A // V7X_LANES

BF16 = jnp.bfloat16
F32 = jnp.float32


def _params(*semantics):
    return pltpu.CompilerParams(dimension_semantics=semantics,
                                vmem_limit_bytes=V7X_VMEM_LIMIT_BYTES)


def _resident(shape, index_map):
    return pl.BlockSpec(shape, index_map, pipeline_mode=pl.Buffered(1))


def _member(shape, lead):
    index = tuple(lead) + (0,) * len(shape)
    return _resident((None,) * len(lead) + tuple(shape), lambda *_: index)


def _rms(x, g):
    ms = jnp.mean(x * x, axis=-1, keepdims=True)
    return x * lax.rsqrt(ms + NORM_EPS) * g


def _dot(a, b):
    return jnp.dot(a, b, preferred_element_type=F32)


def _dot_nt(a, b):
    return lax.dot_general(a, b, (((1,), (1,)), ((), ())), preferred_element_type=F32)


def _dot_tn(a, b):
    return lax.dot_general(a, b, (((0,), (0,)), ((), ())), preferred_element_type=F32)


def _ffn_kernel(x_ref, gpre_ref, wg_ref, wu_ref, wd_ref, gpost_ref, o_ref, acc_ref):
    x = x_ref[...]
    h = _rms(x, gpre_ref[...]).astype(BF16)
    for c in range(D_FF // FFN_COLS):
        cols = slice(c * FFN_COLS, (c + 1) * FFN_COLS)
        gate = _dot(h, wg_ref[:, cols])
        up = _dot(h, wu_ref[:, cols])
        act = (gate * jax.nn.sigmoid(gate) * up).astype(BF16)
        part = _dot(act, wd_ref[cols, :])
        if c == 0:
            acc_ref[...] = part
        else:
            acc_ref[...] += part
    o_ref[...] = x + 0.5 * _rms(acc_ref[...], gpost_ref[...])


def _ffn(x2d, gains, w_gate, w_up, w_down, layer, which):
    t = x2d.shape[0]
    row = lambda i: (i, 0)
    pre, post = FFN_NORMS[which]
    return pl.pallas_call(
        _ffn_kernel,
        out_shape=jax.ShapeDtypeStruct((t, D_MODEL), F32),
        grid=(t // FFN_ROWS,),
        in_specs=[
            pl.BlockSpec((FFN_ROWS, D_MODEL), row),
            _member((1, D_MODEL), (layer, pre)),
            _member((D_MODEL, D_FF), (layer, which)),
            _member((D_MODEL, D_FF), (layer, which)),
            _member((D_FF, D_MODEL), (layer, which)),
            _member((1, D_MODEL), (layer, post)),
        ],
        out_specs=pl.BlockSpec((FFN_ROWS, D_MODEL), row),
        scratch_shapes=[pltpu.VMEM((FFN_ROWS, D_MODEL), F32)],
        compiler_params=_params("parallel"),
        name="ffn",
    )(x2d, gains, w_gate, w_up, w_down, gains)


def _ffn_pair_kernel(x_ref, *refs):
    o_ref, acc_ref = refs[-2], refs[-1]
    x = x_ref[...]
    for stage in range(2):
        gpre_ref, wg_ref, wu_ref, wd_ref, gpost_ref = refs[5 * stage:5 * stage + 5]
        h = _rms(x, gpre_ref[...]).astype(BF16)
        for c in range(D_FF // FFN_COLS):
            cols = slice(c * FFN_COLS, (c + 1) * FFN_COLS)
            gate = _dot(h, wg_ref[:, cols])
            up = _dot(h, wu_ref[:, cols])
            act = (gate * jax.nn.sigmoid(gate) * up).astype(BF16)
            part = _dot(act, wd_ref[cols, :])
            if c == 0:
                acc_ref[...] = part
            else:
                acc_ref[...] += part
        x = x + 0.5 * _rms(acc_ref[...], gpost_ref[...])
    o_ref[...] = x


def _ffn_pair(x2d, gains, w_gate, w_up, w_down, layer):
    t = x2d.shape[0]
    row = lambda i: (i, 0)
    specs, args = [], []
    for lyr, which in ((layer, 1), (layer + 1, 0)):
        pre, post = FFN_NORMS[which]
        specs += [_member((1, D_MODEL), (lyr, pre)), _member((D_MODEL, D_FF), (lyr, which)),
                  _member((D_MODEL, D_FF), (lyr, which)), _member((D_FF, D_MODEL), (lyr, which)),
                  _member((1, D_MODEL), (lyr, post))]
        args += [gains, w_gate, w_up, w_down, gains]
    return pl.pallas_call(
        _ffn_pair_kernel,
        out_shape=jax.ShapeDtypeStruct((t, D_MODEL), F32),
        grid=(t // FFN_PAIR_ROWS,),
        in_specs=[pl.BlockSpec((FFN_PAIR_ROWS, D_MODEL), row)] + specs,
        out_specs=pl.BlockSpec((FFN_PAIR_ROWS, D_MODEL), row),
        scratch_shapes=[pltpu.VMEM((FFN_PAIR_ROWS, D_MODEL), F32)],
        compiler_params=_params("parallel"),
        name="ffn_pair",
    )(x2d, *args)


def _qkv_chunk(h, w_ref, grp, c):
    first = grp * GROUP_COLS + c * PROJ_COLS
    res = _dot(h, w_ref[:, first:first + PROJ_COLS])
    if c * PROJ_COLS < D_A:
        res = res * (DH_A ** -0.5)
    return res.astype(BF16)


def _attn_proj_kernel(x_ref, g_ref, w_ref, o0_ref, o1_ref, o2_ref, slab_ref):
    rows = x_ref.shape[1]
    h = _rms(x_ref[0], g_ref[...])
    for j in range(LANE_SLABS):
        slab_ref[j] = h[:, j * V7X_LANES:(j + 1) * V7X_LANES]
    hb = h.astype(BF16)
    for c in range(GROUP_COLS // PROJ_COLS):
        cols = slice(c * PROJ_COLS, (c + 1) * PROJ_COLS)
        o0_ref[0, 0, :, cols] = _qkv_chunk(hb, w_ref, 0, c)
    for grp, o_ref in ((1, o1_ref), (2, o2_ref)):
        dil = DILATED_GROUPS[grp][1]
        n = rows // dil
        by_residue = [
            jnp.concatenate([slab_ref[j, pl.ds(r, n, stride=dil), :] for j in range(LANE_SLABS)], axis=-1)
            for r in range(dil)]
        hp = jnp.concatenate(by_residue, axis=0).astype(BF16)
        for c in range(GROUP_COLS // PROJ_COLS):
            cols = slice(c * PROJ_COLS, (c + 1) * PROJ_COLS)
            res = _qkv_chunk(hp, w_ref, grp, c)
            for r in range(dil):
                o_ref[0, r, :, cols] = res[r * n:(r + 1) * n, :]


def _attn_proj(x, gains, w_in, layer):
    b, s, _ = x.shape
    shapes, specs = [], []
    for _, dil in DILATED_GROUPS:
        shapes.append(jax.ShapeDtypeStruct((b, dil, s // dil, GROUP_COLS), BF16))
        specs.append(pl.BlockSpec((1, dil, PROJ_ROWS // dil, GROUP_COLS), lambda bi, i: (bi, 0, i, 0)))
    return pl.pallas_call(
        _attn_proj_kernel,
        out_shape=tuple(shapes),
        grid=(b, s // PROJ_ROWS),
        in_specs=[
            pl.BlockSpec((1, PROJ_ROWS, D_MODEL), lambda bi, i: (bi, i, 0)),
            _member((1, D_MODEL), (layer, MIXER_PRE_NORM)),
            _member((D_MODEL, ATTN_IN_COLS), (layer // 2,)),
        ],
        out_specs=tuple(specs),
        scratch_shapes=[pltpu.VMEM((LANE_SLABS, PROJ_ROWS, V7X_LANES), F32)],
        compiler_params=_params("parallel", "parallel"),
        name="attn_proj",
    )(x, gains, w_in)


def _t5_buckets(rel):
    half = NUM_BUCKETS // 2
    max_exact = half // 2
    n = np.abs(rel)
    large = max_exact + (np.log(np.maximum(n, 1) / max_exact)
                         / np.log(REL_MAX_DISTANCE / max_exact) * (half - max_exact)).astype(np.int64)
    large = np.minimum(large, half - 1)
    return ((rel > 0) * half + np.where(n < max_exact, n, large)).astype(np.int32)


def _attn_window(ls):
    return min(ls, ATTN_Q + 2 * ATTN_RADIUS)


def _window_shifts(ls):
    w = _attn_window(ls)
    if w == ls:
        return (0,)
    return (0, -ATTN_RADIUS, ATTN_Q - w)


def _bias_tiles(rel_bias, g, dilation, ls):
    w = _attn_window(ls)
    n = w + ATTN_Q
    buckets = _t5_buckets(np.arange(-ATTN_RADIUS, ATTN_RADIUS + 1) * dilation)
    bias_g = rel_bias[g * H_A:(g + 1) * H_A][:, buckets].astype(F32)
    m = np.arange(n)
    key_minus_query = np.where(m < w, m, m - n)
    tiles = []
    for shift in _window_shifts(ls):
        off = shift + key_minus_query
        valid = np.abs(off) <= ATTN_RADIUS
        idx = np.clip(off + ATTN_RADIUS, 0, 2 * ATTN_RADIUS)
        band = jnp.where(valid[None], bias_g[:, idx], NEG_INF)
        flat = jnp.tile(band, (1, ATTN_Q))[:, :ATTN_Q * (n - 1)]
        tiles.append(flat.reshape(H_A, ATTN_Q, n - 1)[:, :, :w])
    return jnp.stack(tiles, axis=0)


def _stat_lane(pair, sub):
    return DH_A + pair if sub == 0 else pair


def _attn_kernel(q_ref, k_ref, v_ref, bias_ref, o_ref, m_ref, l_ref, *, ls, residues, blocks):
    w = _attn_window(ls)
    n_blocks = ls // ATTN_Q
    i = pl.program_id(2)
    subs = V7X_LANES // DH_A
    heads = [(pair, sub) for pair in range(HEAD_PAIRS) for sub in range(subs)]
    q_lane = lax.broadcasted_iota(jnp.int32, (ATTN_Q, V7X_LANES), 1)
    v_lane = lax.broadcasted_iota(jnp.int32, (w, V7X_LANES), 1)
    q_half = [q_lane < DH_A, q_lane >= DH_A]
    v_half = [v_lane < DH_A, v_lane >= DH_A]

    def window(u):
        if w == ls:
            return 0, 0
        blk = i * blocks + u
        start = pl.multiple_of(jnp.clip(blk * ATTN_Q - ATTN_RADIUS, 0, ls - w), ATTN_RADIUS)
        return start, jnp.where(blk == 0, 0, jnp.where(blk == n_blocks - 1, 2, 1))

    def scores_of(r, u):
        start, variant = window(u)
        q_rows = slice(u * ATTN_Q, (u + 1) * ATTN_Q)
        scores = {}
        for pair, sub in heads:
            cols = slice(pair * V7X_LANES, (pair + 1) * V7X_LANES)
            qp = q_ref[0, r, q_rows, cols]
            qh = jnp.where(q_half[sub], qp, jnp.zeros_like(qp))
            kp = k_ref[0, r, pl.ds(start, w), cols]
            scores[pair, sub] = _dot_nt(qh, kp) + bias_ref[variant, pair * subs + sub]
        return scores

    def finish(r, u, scores):
        start, _ = window(u)
        q_rows = slice(u * ATTN_Q, (u + 1) * ATTN_Q)
        maxima, probs = {}, {}
        for head, s in scores.items():
            maxima[head] = jnp.max(s, axis=-1, keepdims=True)
            probs[head] = jnp.exp(s - maxima[head]).astype(BF16)
        m_tile = jnp.zeros((ATTN_Q, V7X_LANES), F32)
        l_tile = jnp.ones((ATTN_Q, V7X_LANES), F32)
        for pair in range(HEAD_PAIRS):
            cols = slice(pair * V7X_LANES, (pair + 1) * V7X_LANES)
            vp = v_ref[0, r, pl.ds(start, w), cols]
            out = None
            for sub in range(subs):
                vh = jnp.where(v_half[sub], vp, jnp.ones_like(vp))
                pv = _dot(probs[pair, sub], vh)
                here = q_lane == _stat_lane(pair, sub)
                l_tile = jnp.where(here, pv, l_tile)
                m_tile = jnp.where(here, maxima[pair, sub], m_tile)
                out = pv if out is None else jnp.where(q_half[sub], pv, out)
            o_ref[0, r, q_rows, cols] = out.astype(BF16)
        m_ref[0, r, q_rows, :] = m_tile
        l_ref[0, r, q_rows, :] = l_tile

    units = [(r, u) for r in range(residues) for u in range(blocks)]
    pending = scores_of(*units[0])
    for idx, unit in enumerate(units):
        following = scores_of(*units[idx + 1]) if idx + 1 < len(units) else None
        finish(*unit, pending)
        pending = following


def _attn_group(qkv, bias_tiles):
    b, dil, ls, _ = qkv.shape
    w = _attn_window(ls)
    nvar = bias_tiles.shape[0]
    blocks = min(ATTN_UNITS, ls // ATTN_Q)
    residues = ATTN_UNITS // blocks
    rows = blocks * ATTN_Q
    blk = lambda bi, r, i: (bi, r, i, 0)
    return pl.pallas_call(
        functools.partial(_attn_kernel, ls=ls, residues=residues, blocks=blocks),
        out_shape=(jax.ShapeDtypeStruct((b, dil, ls, D_A), BF16),
                   jax.ShapeDtypeStruct((b, dil, ls, V7X_LANES), F32),
                   jax.ShapeDtypeStruct((b, dil, ls, V7X_LANES), F32)),
        grid=(b, dil // residues, ls // rows),
        in_specs=[
            pl.BlockSpec((1, residues, rows, D_A), blk),
            pl.BlockSpec((1, residues, ls, D_A), lambda bi, r, i: (bi, r, 0, 1)),
            pl.BlockSpec((1, residues, ls, D_A), lambda bi, r, i: (bi, r, 0, 2)),
            _resident((nvar, H_A, ATTN_Q, w), lambda bi, r, i: (0, 0, 0, 0)),
        ],
        out_specs=(pl.BlockSpec((1, residues, rows, D_A), blk),
                   pl.BlockSpec((1, residues, rows, V7X_LANES), blk),
                   pl.BlockSpec((1, residues, rows, V7X_LANES), blk)),
        compiler_params=_params("parallel", "parallel", "arbitrary"),
        name=f"attn_group_d{dil}",
    )(qkv, qkv, qkv, bias_tiles)


def _attn_out_kernel(x_ref, o0_ref, o1_ref, o2_ref, m0_ref, m1_ref, m2_ref, l0_ref, l1_ref, l2_ref,
                     expand_ref, w_ref, g_ref, out_ref, onat_ref, snat_ref):
    rows = x_ref.shape[1]
    for slot, (o_ref, m_ref, l_ref) in enumerate(((o1_ref, m1_ref, l1_ref), (o2_ref, m2_ref, l2_ref))):
        dil = o_ref.shape[1]
        n = rows // dil
        for r in range(dil):
            token_rows = pl.ds(r, n, stride=dil)
            snat_ref[slot, 0, token_rows, :] = m_ref[0, r]
            snat_ref[slot, 1, token_rows, :] = l_ref[0, r]
            blk = o_ref[0, r].astype(F32)
            for j in range(LANE_SLABS):
                onat_ref[slot, j, token_rows, :] = blk[:, j * V7X_LANES:(j + 1) * V7X_LANES]
    maxima = [m0_ref[0, 0], snat_ref[0, 0], snat_ref[1, 0]]
    sums = [l0_ref[0, 0], snat_ref[0, 1], snat_ref[1, 1]]
    outs = [o0_ref[0, 0].astype(F32)] + [
        jnp.concatenate([onat_ref[slot, j] for j in range(LANE_SLABS)], axis=-1) for slot in range(2)]
    top = jnp.maximum(jnp.maximum(maxima[0], maxima[1]), maxima[2])
    scale = [jnp.exp(m - top) for m in maxima]
    denom = scale[0] * sums[0] + scale[1] * sums[1] + scale[2] * sums[2]
    merged = None
    for sc, o in zip(scale, outs):
        wt = (sc / denom).astype(BF16)
        wt_wide = _dot(wt, expand_ref[...])
        term = wt_wide * o
        merged = term if merged is None else merged + term
    y = _dot(merged.astype(BF16), w_ref[...])
    out_ref[0] = x_ref[0] + _rms(y, g_ref[...])


def _attn_out(x, outs, maxima, sums, w_out, gains, layer):
    b, s, _ = x.shape
    head_of_col = np.arange(D_A) // DH_A
    lane_of_col = np.array([_stat_lane(h // 2, h % 2) for h in head_of_col])
    expand = jnp.asarray(np.arange(V7X_LANES)[:, None] == lane_of_col[None, :], BF16)
    tok = lambda bi, i: (bi, i, 0)
    grp = lambda bi, i: (bi, 0, i, 0)
    fixed = lambda bi, i: (0, 0)
    dils = [dil for _, dil in DILATED_GROUPS]
    stat_specs = [pl.BlockSpec((1, dil, OUT_ROWS // dil, V7X_LANES), grp) for dil in dils]
    return pl.pallas_call(
        _attn_out_kernel,
        out_shape=jax.ShapeDtypeStruct((b, s, D_MODEL), F32),
        grid=(b, s // OUT_ROWS),
        in_specs=[pl.BlockSpec((1, OUT_ROWS, D_MODEL), tok)]
        + [pl.BlockSpec((1, dil, OUT_ROWS // dil, D_A), grp) for dil in dils]
        + stat_specs + stat_specs
        + [_resident((V7X_LANES, D_A), fixed),
           _member((D_A, D_MODEL), (layer // 2,)),
           _member((1, D_MODEL), (layer, MIXER_POST_NORM))],
        out_specs=pl.BlockSpec((1, OUT_ROWS, D_MODEL), tok),
        scratch_shapes=[pltpu.VMEM((2, LANE_SLABS, OUT_ROWS, V7X_LANES), F32),
                        pltpu.VMEM((2, 2, OUT_ROWS, V7X_LANES), F32)],
        compiler_params=_params("parallel", "parallel"),
        name="attn_out",
    )(x, *outs, *maxima, *sums, expand, w_out, gains)


def _rotary(t, cos, sin):
    half = DK_R // 2
    parts = []
    for h in range(t.shape[1] // DK_R):
        t1 = t[:, h * DK_R:h * DK_R + half]
        t2 = t[:, h * DK_R + half:(h + 1) * DK_R]
        parts += [t1 * cos - t2 * sin, t1 * sin + t2 * cos]
    return jnp.concatenate(parts, axis=-1)


def _ret_proj_kernel(x_ref, g_ref, w_ref, cos_ref, sin_ref, o_ref):
    h = _rms(x_ref[...], g_ref[...]).astype(BF16)
    for c in range(RET_IN_COLS // PROJ_COLS):
        first = c * PROJ_COLS
        cols = slice(first, first + PROJ_COLS)
        res = _dot(h, w_ref[:, cols])
        if first < 2 * RET_QK_COLS:
            res = _rotary(res, cos_ref[...], sin_ref[...])
            if first >= RET_QK_COLS:
                res = res * (DK_R ** -0.5)
        o_ref[:, cols] = res.astype(BF16)


def _ret_proj(x2d, gains, w_in, cos, sin, s, layer):
    t = x2d.shape[0]
    assert RET_QK_COLS % PROJ_COLS == 0 and PROJ_COLS % DK_R == 0
    row = lambda i: (i, 0)
    pos = lambda i: (i % (s // PROJ_ROWS), 0)
    return pl.pallas_call(
        _ret_proj_kernel,
        out_shape=jax.ShapeDtypeStruct((t, RET_IN_COLS), BF16),
        grid=(t // PROJ_ROWS,),
        in_specs=[
            pl.BlockSpec((PROJ_ROWS, D_MODEL), row),
            _member((1, D_MODEL), (layer, MIXER_PRE_NORM)),
            _member((D_MODEL, RET_IN_COLS), (layer // 2,)),
            pl.BlockSpec((PROJ_ROWS, DK_R // 2), pos),
            pl.BlockSpec((PROJ_ROWS, DK_R // 2), pos),
        ],
        out_specs=pl.BlockSpec((PROJ_ROWS, RET_IN_COLS), row),
        compiler_params=_params("parallel"),
        name="ret_proj",
    )(x2d, gains, w_in, cos, sin)


def _group_norm(y):
    mu = jnp.mean(y, axis=-1, keepdims=True)
    d = y - mu
    var = jnp.mean(d * d, axis=-1, keepdims=True)
    return d * lax.rsqrt(var + NORM_EPS)


def _retention_kernel(lg_ref, q_ref, k_ref, v_ref, gf_ref, gb_ref, o_ref, state_ref, y_ref):
    c_len = RET_SCAN_CHUNK
    n_chunks = q_ref.shape[1] // c_len
    head = pl.program_id(1)
    ii = lax.broadcasted_iota(jnp.int32, (c_len, c_len), 0).astype(F32)
    jj = lax.broadcasted_iota(jnp.int32, (c_len, c_len), 1).astype(F32)
    col = lax.broadcasted_iota(jnp.int32, (c_len, 1), 0).astype(F32)
    dirs = (0, 1)
    gate_refs = (gf_ref, gb_ref)
    dmat, q_decay, k_decay, chunk_decay = [], [], [], []
    for d in dirs:
        lg = lg_ref[d, head]
        rel = ii - jj if d == 0 else jj - ii
        dmat.append(jnp.where(rel >= 0, jnp.exp(lg * jnp.maximum(rel, 0.0)), 0.0))
        q_decay.append(jnp.exp(lg * (col + 1.0 if d == 0 else c_len - col)))
        k_decay.append(jnp.exp(lg * (c_len - 1.0 - col if d == 0 else col)))
        chunk_decay.append(jnp.exp(lg * c_len))
    state_ref[...] = jnp.zeros_like(state_ref)

    def step(n, finish):
        rows = [pl.ds(pl.multiple_of(chunk * c_len, c_len), c_len) for chunk in (n, n_chunks - 1 - n)]
        q = [q_ref[0, rows[d], :] for d in dirs]
        k = [k_ref[0, rows[d], :] for d in dirs]
        v = [v_ref[0, rows[d], :] for d in dirs]
        state = [state_ref[d] for d in dirs]
        scores = [_dot_nt(q[d], k[d]) for d in dirs]
        cross = [_dot(q[d], state[d].astype(BF16)) for d in dirs]
        probs = [(scores[d] * dmat[d]).astype(BF16) for d in dirs]
        kd = [(k[d].astype(F32) * k_decay[d]).astype(BF16) for d in dirs]
        inner = [_dot(probs[d], v[d]) for d in dirs]
        update = [_dot_tn(kd[d], v[d]) for d in dirs]
        for d in dirs:
            state_ref[d] = state[d] * chunk_decay[d] + update[d]
        for d in dirs:
            gate = gate_refs[d][0, rows[d], :].astype(F32)
            y = gate * jax.nn.sigmoid(gate) * _group_norm(inner[d] + cross[d] * q_decay[d])
            if finish:
                o_ref[0, rows[d], :] = (y + y_ref[1 - d, rows[d], :]).astype(BF16)
            else:
                y_ref[d, rows[d], :] = y

    def first_half(n, carry):
        step(n, False)
        return carry

    def second_half(n, carry):
        step(n, True)
        return carry

    lax.fori_loop(0, n_chunks // 2, first_half, 0, unroll=RET_UNROLL)
    lax.fori_loop(n_chunks // 2, n_chunks, second_half, 0, unroll=RET_UNROLL)


def _retention(proj, log_gamma):
    b, s, _ = proj.shape
    assert s % (2 * RET_UNROLL * RET_SCAN_CHUNK) == 0
    q0 = 0
    k0 = H_R
    v0 = (2 * RET_QK_COLS) // DV_R
    gf0 = v0 + H_R
    gb0 = gf0 + H_R
    return pl.pallas_call(
        _retention_kernel,
        out_shape=jax.ShapeDtypeStruct((b, s, D_V), BF16),
        grid_spec=pltpu.PrefetchScalarGridSpec(
            num_scalar_prefetch=1,
            grid=(b, H_R),
            in_specs=[
                pl.BlockSpec((1, s, DK_R), lambda bi, h, lg: (bi, 0, q0 + h)),
                pl.BlockSpec((1, s, DK_R), lambda bi, h, lg: (bi, 0, k0 + h)),
                pl.BlockSpec((1, s, DV_R), lambda bi, h, lg: (bi, 0, v0 + h)),
                pl.BlockSpec((1, s, DV_R), lambda bi, h, lg: (bi, 0, gf0 + h)),
                pl.BlockSpec((1, s, DV_R), lambda bi, h, lg: (bi, 0, gb0 + h)),
            ],
            out_specs=pl.BlockSpec((1, s, DV_R), lambda bi, h, lg: (bi, 0, h)),
            scratch_shapes=[pltpu.VMEM((2, DK_R, DV_R), F32), pltpu.VMEM((2, s, DV_R), F32)],
        ),
        compiler_params=_params("parallel", "parallel"),
        name="retention",
    )(log_gamma, proj, proj, proj, proj, proj)


def _proj_out_kernel(x_ref, y_ref, w_ref, g_ref, out_ref):
    out_ref[...] = x_ref[...] + _rms(_dot(y_ref[...], w_ref[...]), g_ref[...])


def _proj_out(x2d, y2d, w_out, gains, layer):
    t = x2d.shape[0]
    k = y2d.shape[1]
    row = lambda i: (i, 0)
    return pl.pallas_call(
        _proj_out_kernel,
        out_shape=jax.ShapeDtypeStruct((t, D_MODEL), F32),
        grid=(t // RET_OUT_ROWS,),
        in_specs=[
            pl.BlockSpec((RET_OUT_ROWS, D_MODEL), row),
            pl.BlockSpec((RET_OUT_ROWS, k), row),
            _member((k, D_MODEL), (layer // 2,)),
            _member((1, D_MODEL), (layer, MIXER_POST_NORM)),
        ],
        out_specs=pl.BlockSpec((RET_OUT_ROWS, D_MODEL), row),
        compiler_params=_params("parallel"),
        name="proj_out",
    )(x2d, y2d, w_out, gains)


def _rope_tables(s):
    half = DK_R // 2
    inv_freq = 1.0 / (ROPE_BASE ** jnp.linspace(0.0, 1.0, half, dtype=F32))
    ang = jnp.arange(s, dtype=F32)[:, None] * inv_freq[None, :]
    return jnp.cos(ang), jnp.sin(ang)


def kernel(x, norm_gains, ffn_w_gate, ffn_w_up, ffn_w_down, attn_w_in, attn_w_out,
           rel_bias, ret_w_in, ret_w_out, ret_decay_logit):
    b, s, d = x.shape
    t = b * s
    gains = norm_gains.astype(F32).reshape(DEPTH, -1, 1, D_MODEL)
    w_gate = ffn_w_gate.astype(BF16)
    w_up = ffn_w_up.astype(BF16)
    w_down = ffn_w_down.astype(BF16)
    a_w_in = attn_w_in.astype(BF16)
    a_w_out = attn_w_out.astype(BF16)
    r_w_in = ret_w_in.astype(BF16)
    r_w_out = ret_w_out.astype(BF16)
    bias_tiles = [_bias_tiles(rel_bias, g, dil, s // dil)
                  for g, (_, dil) in enumerate(DILATED_GROUPS)]
    cos, sin = _rope_tables(s)
    log_gamma = jnp.log1p(-jnp.exp(ret_decay_logit.astype(F32)))

    x2d = _ffn(x.reshape(t, d), gains, w_gate, w_up, w_down, 0, 0)
    for i in range(DEPTH):
        if i % 2 == 0:
            x3d = x2d.reshape(b, s, d)
            qkvs = _attn_proj(x3d, gains, a_w_in, i)
            outs, maxima, sums = zip(*[_attn_group(qkv, bt) for qkv, bt in zip(qkvs, bias_tiles)])
            x2d = _attn_out(x3d, outs, maxima, sums, a_w_out, gains, i).reshape(t, d)
        else:
            proj = _ret_proj(x2d, gains, r_w_in, cos, sin, s, i).reshape(b, s, RET_IN_COLS)
            y = _retention(proj, log_gamma[i // 2])
            x2d = _proj_out(x2d, y.reshape(t, D_V), r_w_out, gains, i)
        if i + 1 < DEPTH:
            x2d = _ffn_pair(x2d, gains, w_gate, w_up, w_down, i)
        else:
            x2d = _ffn(x2d, gains, w_gate, w_up, w_down, i, 1)
    return x2d.reshape(b, s, d)
```
